```python
import math
import jax
import jax.numpy as jnp
from jax import lax
import numpy as np

D_MODEL = 2048
BATCH = 8
SEQ = 2048
DEPTH = 4

RMS_EPS = 1e-6
NEG_INF = -1e30
N_BRANCH = 3
SSM_WIDTH = D_MODEL // 4
SSM_GROUP = 16
SSM_GROUPS = SSM_WIDTH // SSM_GROUP
SSM_STATE = 64
DT_MIN = 1e-3
DT_MAX = 1e-1
HEAD_DIM = 64
DSWA_PATTERNS = ((128, 1), (512, 4), (2048, 16))
ATTN_WIDTH = D_MODEL // 4
HEADS_PER_PATTERN = ATTN_WIDTH // HEAD_DIM
N_ATTN_HEADS = HEADS_PER_PATTERN * len(DSWA_PATTERNS)
QKV_WIDTH = N_ATTN_HEADS * HEAD_DIM
CONV_WIDTH = D_MODEL // 4
CONV_K = 3
D_FF = 256 * ((8 * D_MODEL // 3 + 255) // 256)
FFN_CONV_K = 3
OFF_Q = SSM_WIDTH
OFF_K = OFF_Q + QKV_WIDTH
OFF_V = OFF_K + QKV_WIDTH
OFF_CONV = OFF_V + QKV_WIDTH
OFF_GATE = OFF_CONV + 3 * CONV_WIDTH
N_IN = OFF_GATE + N_BRANCH * D_MODEL

kernel_name = 'hybrid_ssm_dilated_attn_conv_trunk'


def alibi_slopes(n_heads):
    return np.array([2.0 ** (-8.0 * (h + 1) / n_heads) for h in range(n_heads)], dtype=np.float32)


def rms_norm(x, g):
    x32 = x.astype(jnp.float32)
    y = x32 * lax.rsqrt(jnp.mean(x32 * x32, axis=-1, keepdims=True) + RMS_EPS)
    return y.astype(x.dtype) * g


def causal_dwconv(z, w):
    k_width = w.shape[0]
    seq = z.shape[1]
    zp = jnp.pad(z, ((0, 0), (k_width - 1, 0), (0, 0)))
    return sum(w[k] * zp[:, k_width - 1 - k:k_width - 1 - k + seq] for k in range(k_width))


def s5_mixer(u, log_dt, a_re, a_im, b_re, b_im, c_re, c_im, d_skip, w_glu, b_glu):
    f32 = jnp.float32
    bsz, seq, _ = u.shape
    u32 = u.astype(f32)
    ug = u32.reshape(bsz, seq, SSM_GROUPS, SSM_GROUP)
    dt = jnp.exp(log_dt.astype(f32))[:, None]
    ar, ai = a_re.astype(f32), a_im.astype(f32)
    mag = jnp.exp(ar * dt)
    lr, li = mag * jnp.cos(ai * dt), mag * jnp.sin(ai * dt)
    den = ar * ar + ai * ai
    fr = ((lr - 1.0) * ar + li * ai) / den
    fi = (li * ar - (lr - 1.0) * ai) / den
    br, bi = b_re.astype(f32), b_im.astype(f32)
    bbr = fr[..., None] * br - fi[..., None] * bi
    bbi = fr[..., None] * bi + fi[..., None] * br
    xr = jnp.einsum('blgc,gnc->blgn', ug, bbr)
    xi = jnp.einsum('blgc,gnc->blgn', ug, bbi)
    lam_r = jnp.broadcast_to(lr, xr.shape)
    lam_i = jnp.broadcast_to(li, xr.shape)

    def combine(e1, e2):
        a1r, a1i, b1r, b1i = e1
        a2r, a2i, b2r, b2i = e2
        return (a2r * a1r - a2i * a1i,
                a2r * a1i + a2i * a1r,
                a2r * b1r - a2i * b1i + b2r,
                a2r * b1i + a2i * b1r + b2i)

    _, _, hr, hi = lax.associative_scan(combine, (lam_r, lam_i, xr, xi), axis=1)
    y = (jnp.einsum('blgn,gcn->blgc', hr, c_re.astype(f32))
         - jnp.einsum('blgn,gcn->blgc', hi, c_im.astype(f32)))
    y = y.reshape(bsz, seq, SSM_WIDTH) + d_skip.astype(f32) * u32
    g = jax.nn.gelu(y).astype(u.dtype)
    return g * jax.nn.sigmoid(g @ w_glu + b_glu)


def dilated_window_attention(q, k, v, slopes, window, dilation):
    f32 = jnp.float32
    bsz, seq, n_h, e = q.shape
    w_steps = window // dilation
    qb_len = w_steps
    unit = dilation * qb_len
    seq_p = -(-seq // unit) * unit
    pad = seq_p - seq
    m_len = seq_p // dilation
    nb = m_len // qb_len

    def to_blocks(t):
        t = jnp.pad(t, ((0, 0), (0, pad), (0, 0), (0, 0)))
        t = t.reshape(bsz, m_len, dilation, n_h, e).transpose(0, 2, 1, 3, 4)
        return t.reshape(bsz, dilation, nb, qb_len, n_h, e)

    def with_prev(t):
        prev = jnp.pad(t[:, :, :-1], ((0, 0), (0, 0), (1, 0), (0, 0), (0, 0), (0, 0)))
        return jnp.concatenate([prev, t], axis=3)

    qb = to_blocks(q)
    kk = with_prev(to_blocks(k))
    vv = with_prev(to_blocks(v))
    s = jnp.einsum('brnqhe,brnkhe->brnhqk', qb, kk).astype(f32) * (e ** -0.5)
    qi = jnp.arange(qb_len)[:, None]
    kj = jnp.arange(2 * qb_len)[None, :]
    dist = qb_len + qi - kj
    blk = jnp.arange(nb)[:, None, None]
    valid = (dist >= 0) & (dist <= w_steps) & ((blk > 0) | (kj >= qb_len))
    bias = -slopes[:, None, None] * (dist * dilation).astype(f32)
    s = jnp.where(valid[:, None], s + bias, NEG_INF)
    m = jnp.max(s, axis=-1, keepdims=True)
    p = jnp.exp(s - m)
    l = jnp.sum(p, axis=-1, keepdims=True)
    o = jnp.einsum('brnhqk,brnkhe->brnqhe', p / l, vv.astype(f32))
    lse = (m + jnp.log(l))[..., 0]
    o = o.reshape(bsz, dilation, m_len, n_h, e).transpose(0, 2, 1, 3, 4).reshape(bsz, seq_p, n_h, e)
    lse = lse.transpose(0, 1, 2, 4, 3).reshape(bsz, dilation, m_len, n_h)
    lse = lse.transpose(0, 2, 1, 3).reshape(bsz, seq_p, n_h)
    return o[:, :seq], lse[:, :seq]


def dilated_attention_mixer(q, k, v):
    bsz, seq = q.shape[0], q.shape[1]
    slopes = jnp.asarray(alibi_slopes(N_ATTN_HEADS))
    outs, lses = [], []
    for g, (window, dilation) in enumerate(DSWA_PATTERNS):
        hs = slice(g * HEADS_PER_PATTERN, (g + 1) * HEADS_PER_PATTERN)
        o, lse = dilated_window_attention(q[:, :, hs], k[:, :, hs], v[:, :, hs], slopes[hs], window, dilation)
        outs.append(o)
        lses.append(lse)
    alpha = jax.nn.softmax(jnp.stack(lses, axis=0), axis=0)
    o = jnp.sum(alpha[..., None] * jnp.stack(outs, axis=0), axis=0)
    return o.reshape(bsz, seq, ATTN_WIDTH).astype(q.dtype)


def hybrid_mixer(h, w_in, ssm_log_dt, ssm_a_re, ssm_a_im, ssm_b_re, ssm_b_im, ssm_c_re, ssm_c_im,
                 ssm_d, w_glu, b_glu, conv_mix_w, w_ssm_out, w_attn_out, w_conv_out, b_gate, w_o):
    bsz, seq, _ = h.shape
    proj = h @ w_in
    u = proj[..., :OFF_Q]
    q = proj[..., OFF_Q:OFF_K].reshape(bsz, seq, N_ATTN_HEADS, HEAD_DIM)
    k = proj[..., OFF_K:OFF_V].reshape(bsz, seq, N_ATTN_HEADS, HEAD_DIM)
    v = proj[..., OFF_V:OFF_CONV].reshape(bsz, seq, N_ATTN_HEADS, HEAD_DIM)
    conv_b, conv_c, conv_h = jnp.split(proj[..., OFF_CONV:OFF_GATE], 3, axis=-1)
    gates = jax.nn.sigmoid(proj[..., OFF_GATE:] + b_gate).reshape(bsz, seq, N_BRANCH, D_MODEL)
    y_ssm = s5_mixer(u, ssm_log_dt, ssm_a_re, ssm_a_im, ssm_b_re, ssm_b_im, ssm_c_re, ssm_c_im,
                     ssm_d, w_glu, b_glu) @ w_ssm_out
    y_attn = dilated_attention_mixer(q, k, v) @ w_attn_out
    y_conv = (conv_b * causal_dwconv(conv_c * conv_h, conv_mix_w)) @ w_conv_out
    merged = gates[:, :, 0] * y_ssm + gates[:, :, 1] * y_attn + gates[:, :, 2] * y_conv
    return merged @ w_o


def conv_ffn(h, w_up, ffn_conv_w, w_down):
    up = causal_dwconv(h @ w_up, ffn_conv_w)
    a, b = jnp.split(up, 2, axis=-1)
    return (jax.nn.silu(a) * b) @ w_down


def setup_inputs(seed: int = 0) -> dict:
    key = jax.random.key(seed)
    ks = jax.random.split(key, 32)
    f32 = jnp.float32

    def nrm(k, shape, scale):
        return scale * jax.random.normal(k, shape, f32)

    nl = DEPTH
    n_idx = jnp.arange(SSM_STATE, dtype=f32)
    return {
        'x': nrm(ks[0], (BATCH, SEQ, D_MODEL), 1.0),
        'c': nrm(ks[1], (BATCH, D_MODEL), 1.0),
        'w_mod': nrm(ks[2], (nl, D_MODEL, 6 * D_MODEL), 0.5 * D_MODEL ** -0.5),
        'b_mod': nrm(ks[3], (nl, 6 * D_MODEL), 0.01),
        'g_pre_mix': 1.0 + nrm(ks[4], (nl, D_MODEL), 0.02),
        'g_post_mix': 1.0 + nrm(ks[5], (nl, D_MODEL), 0.02),
        'g_pre_ffn': 1.0 + nrm(ks[6], (nl, D_MODEL), 0.02),
        'g_post_ffn': 1.0 + nrm(ks[7], (nl, D_MODEL), 0.02),
        'w_in': nrm(ks[8], (nl, D_MODEL, N_IN), D_MODEL ** -0.5),
        'ssm_log_dt': jax.random.uniform(ks[9], (nl, SSM_GROUPS), f32, math.log(DT_MIN), math.log(DT_MAX)),
        'ssm_a_re': -0.5 + nrm(ks[10], (nl, SSM_GROUPS, SSM_STATE), 0.01),
        'ssm_a_im': jnp.pi * n_idx + nrm(ks[11], (nl, SSM_GROUPS, SSM_STATE), 0.01),
        'ssm_b_re': nrm(ks[12], (nl, SSM_GROUPS, SSM_STATE, SSM_GROUP), (2 * SSM_GROUP) ** -0.5),
        'ssm_b_im': nrm(ks[13], (nl, SSM_GROUPS, SSM_STATE, SSM_GROUP), (2 * SSM_GROUP) ** -0.5),
        'ssm_c_re': nrm(ks[14], (nl, SSM_GROUPS, SSM_GROUP, SSM_STATE), 0.5),
        'ssm_c_im': nrm(ks[15], (nl, SSM_GROUPS, SSM_GROUP, SSM_STATE), 0.5),
        'ssm_d': nrm(ks[16], (nl, SSM_WIDTH), 1.0),
        'w_glu': nrm(ks[17], (nl, SSM_WIDTH, SSM_WIDTH), SSM_WIDTH ** -0.5),
        'b_glu': nrm(ks[18], (nl, SSM_WIDTH), 0.01),
        'conv_mix_w': nrm(ks[19], (nl, CONV_K, CONV_WIDTH), CONV_K ** -0.5),
        'w_ssm_out': nrm(ks[20], (nl, SSM_WIDTH, D_MODEL), SSM_WIDTH ** -0.5),
        'w_attn_out': nrm(ks[21], (nl, ATTN_WIDTH, D_MODEL), ATTN_WIDTH ** -0.5),
        'w_conv_out': nrm(ks[22], (nl, CONV_WIDTH, D_MODEL), CONV_WIDTH ** -0.5),
        'b_gate': nrm(ks[23], (nl, N_BRANCH * D_MODEL), 0.01),
        'w_o': nrm(ks[24], (nl, D_MODEL, D_MODEL), D_MODEL ** -0.5),
        'w_up': nrm(ks[25], (nl, D_MODEL, 2 * D_FF), D_MODEL ** -0.5),
        'ffn_conv_w': nrm(ks[26], (nl, FFN_CONV_K, 2 * D_FF), FFN_CONV_K ** -0.5),
        'w_down': nrm(ks[27], (nl, D_FF, D_MODEL), D_FF ** -0.5),
    }


def reference(x, c, w_mod, b_mod, g_pre_mix, g_post_mix, g_pre_ffn, g_post_ffn, w_in,
              ssm_log_dt, ssm_a_re, ssm_a_im, ssm_b_re, ssm_b_im, ssm_c_re, ssm_c_im, ssm_d,
              w_glu, b_glu, conv_mix_w, w_ssm_out, w_attn_out, w_conv_out, b_gate, w_o,
              w_up, ffn_conv_w, w_down):
    cond = jax.nn.silu(c)
    for l in range(DEPTH):
        mod = cond @ w_mod[l] + b_mod[l]
        sh1, sc1, gt1, sh2, sc2, gt2 = jnp.split(mod, 6, axis=-1)
        h = rms_norm(x, g_pre_mix[l]) * (1.0 + sc1[:, None]) + sh1[:, None]
        y = hybrid_mixer(h, w_in[l], ssm_log_dt[l], ssm_a_re[l], ssm_a_im[l], ssm_b_re[l], ssm_b_im[l],
                         ssm_c_re[l], ssm_c_im[l], ssm_d[l], w_glu[l], b_glu[l], conv_mix_w[l],
                         w_ssm_out[l], w_attn_out[l], w_conv_out[l], b_gate[l], w_o[l])
        x = x + gt1[:, None] * rms_norm(y, g_post_mix[l])
        h = rms_norm(x, g_pre_ffn[l]) * (1.0 + sc2[:, None]) + sh2[:, None]
        y = conv_ffn(h, w_up[l], ffn_conv_w[l], w_down[l])
        x = x + gt2[:, None] * rms_norm(y, g_post_ffn[l])
    return x
```

```python
import functools

import numpy as np
import jax
import jax.numpy as jnp
from jax import lax
from jax.experimental import pallas as pl
from jax.experimental.pallas import tpu as pltpu

F32 = jnp.float32
BF16 = jnp.bfloat16

D_MODEL = 2048
BATCH = 8
SEQ = 2048
DEPTH = 4
TOKENS = BATCH * SEQ
RMS_EPS = 1e-6
NEG_INF = -1e30
N_BRANCH = 3
SSM_WIDTH = 512
SSM_GROUP = 16
SSM_GROUPS = 32
SSM_STATE = 64
HEAD_DIM = 64
DSWA_PATTERNS = ((128, 1), (512, 4), (2048, 16))
ATTN_WIDTH = 512
HEADS_PER_PATTERN = 8
N_ATTN_HEADS = 24
QKV_WIDTH = 1536
CONV_WIDTH = 512
D_FF = 5632
N_IN = 12800

P_GATE = 0
P_U = N_BRANCH * D_MODEL
P_CONV = P_U + SSM_WIDTH
P_Q = P_CONV + 3 * CONV_WIDTH
P_K = P_Q + QKV_WIDTH
P_V = P_K + QKV_WIDTH

LANES = 128
VMEM_LIMIT = 56 * 1024 * 1024

TM_PROJ = 1024
TN_PROJ = 1280
TF_UP = 512
HALO = 16
TM_MERGE = 256
TM_DOWN = 256
SSM_CHUNK = 128
QBLK = 128


def _cparams(sem):
    return pltpu.CompilerParams(dimension_semantics=sem, vmem_limit_bytes=VMEM_LIMIT)


def _resident(shape, index_map):
    return pl.BlockSpec(shape, index_map, pipeline_mode=pl.Buffered(1))


def _mod_kernel(c_ref, w_ref, b_ref, o_ref):
    c = c_ref[...]
    cond = (c * jax.nn.sigmoid(c)).astype(BF16)
    o_ref[0] = jnp.dot(cond, w_ref[0].astype(BF16), preferred_element_type=F32) + b_ref[0]


def _modulation(c, w_mod, b_mod):
    tn = 1024
    n = 6 * D_MODEL
    return pl.pallas_call(
        _mod_kernel,
        out_shape=jax.ShapeDtypeStruct((DEPTH, BATCH, n), F32),
        grid=(DEPTH, n // tn),
        in_specs=[
            pl.BlockSpec((BATCH, D_MODEL), lambda l, j: (0, 0)),
            pl.BlockSpec((1, D_MODEL, tn), lambda l, j: (l, 0, j)),
            pl.BlockSpec((1, 1, tn), lambda l, j: (l, 0, j)),
        ],
        out_specs=pl.BlockSpec((1, BATCH, tn), lambda l, j: (l, 0, j)),
        compiler_params=_cparams(("arbitrary", "arbitrary")),
        name="modulation",
    )(c, w_mod, b_mod.reshape(DEPTH, 1, n))


def _norm_mod_rows(x_ref, g_ref, sh_ref, sc_ref, b, h_scr, dst_row, n_rows):
    rows = 128 if n_rows % 128 == 0 else n_rows
    g = g_ref[...]
    scale = 1.0 + sc_ref[pl.ds(b, 1), :]
    shift = sh_ref[pl.ds(b, 1), :]

    def body(k, carry):
        r0 = pl.multiple_of(k * rows, rows)
        x = x_ref[pl.ds(r0, rows), :]
        ms = jnp.mean(x * x, axis=-1, keepdims=True)
        y = x * lax.rsqrt(ms + RMS_EPS) * g
        h_scr[pl.ds(dst_row + r0, rows), :] = (y * scale + shift).astype(BF16)
        return carry

    lax.fori_loop(0, n_rows // rows, body, 0)


def _inproj_kernel(x_ref, g_ref, sh_ref, sc_ref, w_ref, bias_ref, o_ref, h_scr):
    i = pl.program_id(0)
    j = pl.program_id(1)

    @pl.when(j == 0)
    def _():
        _norm_mod_rows(x_ref, g_ref, sh_ref, sc_ref, i // (SEQ // TM_PROJ), h_scr, 0, TM_PROJ)

    acc = jnp.dot(h_scr[...], w_ref[...], preferred_element_type=F32)
    full_gate_tiles = P_U // TN_PROJ
    gate_cols_in_mixed = P_U - full_gate_tiles * TN_PROJ

    @pl.when(j < full_gate_tiles)
    def _():
        o_ref[...] = jax.nn.sigmoid(acc + bias_ref[...])

    @pl.when(j == full_gate_tiles)
    def _():
        col = lax.broadcasted_iota(jnp.int32, acc.shape, 1)
        o_ref[...] = jnp.where(col < gate_cols_in_mixed, jax.nn.sigmoid(acc + bias_ref[...]), acc)

    @pl.when(j > full_gate_tiles)
    def _():
        o_ref[...] = acc


def _in_projection(x, g, mod_l, w_in_p, bias_p):
    d = D_MODEL
    return pl.pallas_call(
        _inproj_kernel,
        out_shape=jax.ShapeDtypeStruct((TOKENS, N_IN), F32),
        grid=(TOKENS // TM_PROJ, N_IN // TN_PROJ),
        in_specs=[
            pl.BlockSpec((TM_PROJ, d), lambda i, j: (i, 0)),
            pl.BlockSpec((1, d), lambda i, j: (0, 0)),
            pl.BlockSpec((BATCH, d), lambda i, j: (0, 0)),
            pl.BlockSpec((BATCH, d), lambda i, j: (0, 1)),
            pl.BlockSpec((d, TN_PROJ), lambda i, j: (0, j)),
            pl.BlockSpec((1, TN_PROJ), lambda i, j: (0, j)),
        ],
        out_specs=pl.BlockSpec((TM_PROJ, TN_PROJ), lambda i, j: (i, j)),
        scratch_shapes=[pltpu.VMEM((TM_PROJ, d), BF16)],
        compiler_params=_cparams(("arbitrary", "arbitrary")),
        name="in_projection",
    )(x, g, mod_l, mod_l, w_in_p, bias_p)


def _ssm_param_kernel(ldt_ref, ar_ref, ai_ref, br_ref, bi_ref, lr_ref, li_ref, bbr_ref, bbi_ref):
    dt = jnp.exp(ldt_ref[...])
    ar = ar_ref[...]
    ai = ai_ref[...]
    mag = jnp.exp(ar * dt)
    lr = mag * jnp.cos(ai * dt)
    li = mag * jnp.sin(ai * dt)
    den = ar * ar + ai * ai
    fr = ((lr - 1.0) * ar + li * ai) / den
    fi = (li * ar - (lr - 1.0) * ai) / den
    lr_ref[...] = lr
    li_ref[...] = li
    for ch in range(SSM_GROUP):
        br = br_ref[ch]
        bi = bi_ref[ch]
        bbr_ref[ch] = fr * br - fi * bi
        bbi_ref[ch] = fr * bi + fi * br


def _ssm_params(log_dt, a_re, a_im, b_re, b_im):
    n = DEPTH * SSM_GROUPS
    mat = jax.ShapeDtypeStruct((n, SSM_STATE), F32)
    cube = jax.ShapeDtypeStruct((SSM_GROUP, n, SSM_STATE), F32)

    def channel_major(b):
        return b.transpose(3, 0, 1, 2).reshape(SSM_GROUP, n, SSM_STATE)

    lr, li, bbr, bbi = pl.pallas_call(
        _ssm_param_kernel,
        out_shape=(mat, mat, cube, cube),
        name="ssm_params",
    )(log_dt.reshape(n, 1), a_re.reshape(n, SSM_STATE), a_im.reshape(n, SSM_STATE),
      channel_major(b_re), channel_major(b_im))
    shape3 = (DEPTH, SSM_GROUPS, SSM_STATE)
    return (lr.reshape(shape3), li.reshape(shape3),
            bbr.reshape((SSM_GROUP,) + shape3), bbi.reshape((SSM_GROUP,) + shape3))


N_SLAB = SSM_WIDTH // LANES
GROUPS_PER_SLAB = LANES // SSM_GROUP
STATES_PER_SLAB = GROUPS_PER_SLAB * SSM_STATE
N_STATES = SSM_GROUPS * SSM_STATE


def _ssm_layout(lr, li, bbr, bbi, c_re, c_im):
    eye = jnp.eye(GROUPS_PER_SLAB, dtype=F32)

    def in_blocks(bb):
        t = bb.reshape(SSM_GROUP, N_SLAB, GROUPS_PER_SLAB, SSM_STATE).transpose(1, 2, 0, 3)
        blk = eye[None, :, None, :, None] * t[:, :, :, None, :]
        return blk.reshape(N_SLAB, LANES, STATES_PER_SLAB)

    def out_blocks(cc):
        t = cc.reshape(N_SLAB, GROUPS_PER_SLAB, SSM_GROUP, SSM_STATE).transpose(0, 1, 3, 2)
        blk = eye[None, :, None, :, None] * t[:, :, :, None, :]
        return blk.reshape(N_SLAB, STATES_PER_SLAB, LANES)

    bb = jnp.concatenate([in_blocks(bbr), in_blocks(bbi)], axis=-1).astype(BF16)
    lam = jnp.stack([lr.reshape(N_STATES), li.reshape(N_STATES)], axis=0)
    return bb, out_blocks(c_re).astype(BF16), out_blocks(c_im).astype(BF16), lam


def _ssm_kernel(u_ref, perm_ref, permt_ref, bb_ref, cre_ref, cim_ref, lam_ref, dsk_ref, wglu_ref, bglu_ref,
                o_ref, x_scr, h_scr):
    rows = BATCH * SSM_CHUNK

    @pl.when(pl.program_id(0) == 0)
    def _():
        h_scr[...] = jnp.zeros_like(h_scr)

    u = u_ref[...].reshape(rows, SSM_WIDTH).astype(BF16)
    u_tm = jnp.dot(perm_ref[...], u, preferred_element_type=F32)
    u_tm_b = u_tm.astype(BF16)

    for s in range(N_SLAB):
        xs = jnp.dot(u_tm_b[:, s * LANES:(s + 1) * LANES], bb_ref[s], preferred_element_type=F32)
        x_scr[:, s * STATES_PER_SLAB:(s + 1) * STATES_PER_SLAB] = xs[:, :STATES_PER_SLAB]
        x_scr[:, N_STATES + s * STATES_PER_SLAB:N_STATES + (s + 1) * STATES_PER_SLAB] = xs[:, STATES_PER_SLAB:]

    width = STATES_PER_SLAB
    for s in range(N_STATES // width):
        re_cols = pl.ds(s * width, width)
        im_cols = pl.ds(N_STATES + s * width, width)
        lr = jnp.broadcast_to(lam_ref[0:1, s * width:(s + 1) * width], (BATCH, width))
        li = jnp.broadcast_to(lam_ref[1:2, s * width:(s + 1) * width], (BATCH, width))

        def step(t, carry, re_cols=re_cols, im_cols=im_cols, lr=lr, li=li):
            hr, hi = carry
            r0 = pl.multiple_of(t * BATCH, BATCH)
            xr = x_scr[pl.ds(r0, BATCH), re_cols]
            xi = x_scr[pl.ds(r0, BATCH), im_cols]
            nr = lr * hr - li * hi + xr
            ni = lr * hi + li * hr + xi
            x_scr[pl.ds(r0, BATCH), re_cols] = nr
            x_scr[pl.ds(r0, BATCH), im_cols] = ni
            return nr, ni

        hr, hi = lax.fori_loop(0, SSM_CHUNK, step, (h_scr[:, re_cols], h_scr[:, im_cols]), unroll=4)
        h_scr[:, re_cols] = hr
        h_scr[:, im_cols] = hi

    ys = []
    for s in range(N_SLAB):
        h_re = x_scr[:, s * STATES_PER_SLAB:(s + 1) * STATES_PER_SLAB].astype(BF16)
        h_im = x_scr[:, N_STATES + s * STATES_PER_SLAB:N_STATES + (s + 1) * STATES_PER_SLAB].astype(BF16)
        ys.append(jnp.dot(h_re, cre_ref[s], preferred_element_type=F32)
                  - jnp.dot(h_im, cim_ref[s], preferred_element_type=F32))
    y = jnp.concatenate(ys, axis=-1) + dsk_ref[...] * u_tm
    g = jax.nn.gelu(y)
    gate = jnp.dot(g.astype(BF16), wglu_ref[...], preferred_element_type=F32) + bglu_ref[...]
    out_tm = (g * jax.nn.sigmoid(gate)).astype(BF16)
    out = jnp.dot(permt_ref[...], out_tm, preferred_element_type=F32)
    o_ref[...] = out.astype(BF16).reshape(BATCH, SSM_CHUNK, SSM_WIDTH)


def _ssm_mixer(proj3, perm, permt, bb, cre, cim, lam, d_skip, w_glu, b_glu):
    rows = BATCH * SSM_CHUNK
    return pl.pallas_call(
        _ssm_kernel,
        out_shape=jax.ShapeDtypeStruct((BATCH, SEQ, SSM_WIDTH), BF16),
        grid=(SEQ // SSM_CHUNK,),
        in_specs=[
            pl.BlockSpec((BATCH, SSM_CHUNK, SSM_WIDTH), lambda c: (0, c, P_U // SSM_WIDTH)),
            _resident((rows, rows), lambda c: (0, 0)),
            _resident((rows, rows), lambda c: (0, 0)),
            _resident((N_SLAB, LANES, 2 * STATES_PER_SLAB), lambda c: (0, 0, 0)),
            _resident((N_SLAB, STATES_PER_SLAB, LANES), lambda c: (0, 0, 0)),
            _resident((N_SLAB, STATES_PER_SLAB, LANES), lambda c: (0, 0, 0)),
            _resident((2, N_STATES), lambda c: (0, 0)),
            _resident((1, SSM_WIDTH), lambda c: (0, 0)),
            _resident((SSM_WIDTH, SSM_WIDTH), lambda c: (0, 0)),
            _resident((1, SSM_WIDTH), lambda c: (0, 0)),
        ],
        out_specs=pl.BlockSpec((BATCH, SSM_CHUNK, SSM_WIDTH), lambda c: (0, c, 0)),
        scratch_shapes=[pltpu.VMEM((rows, 2 * N_STATES), F32), pltpu.VMEM((BATCH, 2 * N_STATES), F32)],
        compiler_params=_cparams(("arbitrary",)),
        name="ssm_mixer",
    )(proj3, perm, permt, bb, cre, cim, lam, d_skip, w_glu, b_glu)


def _attn_kernel(slopes_ref, q0, k0, v0, q1, k1, v1, q2, k2, v2, o_ref,
                 o0, l0, o1, l1, o2, l2):
    pair = pl.program_id(1)
    lane = lax.broadcasted_iota(jnp.int32, (QBLK, LANES), 1)
    head0 = lane < HEAD_DIM
    qi = lax.broadcasted_iota(jnp.int32, (QBLK, QBLK), 0)
    kj = lax.broadcasted_iota(jnp.int32, (QBLK, QBLK), 1)
    dist_prev = (QBLK + qi - kj).astype(F32)
    dist_cur = (qi - kj).astype(F32)
    valid_prev = kj >= qi
    valid_cur = kj <= qi
    contract_last = (((1,), (1,)), ((), ()))

    groups = ((q0, k0, v0, o0, l0), (q1, k1, v1, o1, l1), (q2, k2, v2, o2, l2))
    for g, (_, dil) in enumerate(DSWA_PATTERNS):
        q_ref, k_ref, v_ref, o_scr, l_scr = groups[g]
        n_blocks = SEQ // (dil * QBLK)
        slope_a = slopes_ref[g * HEADS_PER_PATTERN + 2 * pair] * float(dil)
        slope_b = slopes_ref[g * HEADS_PER_PATTERN + 2 * pair + 1] * float(dil)

        def rows(start, dil=dil):
            if dil == 1:
                return pl.ds(pl.multiple_of(start, QBLK), QBLK)
            return pl.ds(start, QBLK, stride=dil)

        def body(idx, carry, q_ref=q_ref, k_ref=k_ref, v_ref=v_ref, o_scr=o_scr, l_scr=l_scr,
                 n_blocks=n_blocks, dil=dil, slope_a=slope_a, slope_b=slope_b, rows=rows):
            res = idx // n_blocks
            blk = idx % n_blocks
            cur = rows(dil * QBLK * blk + res)
            prev = rows(dil * QBLK * jnp.maximum(blk - 1, 0) + res)
            has_prev = blk > 0
            q = q_ref[cur, :] * (HEAD_DIM ** -0.5)
            k_cur = k_ref[cur, :].astype(BF16)
            v_cur = v_ref[cur, :].astype(BF16)
            k_prev = k_ref[prev, :].astype(BF16)
            v_prev = v_ref[prev, :].astype(BF16)
            outs, lses = [], []
            for is_a, slope in ((True, slope_a), (False, slope_b)):
                qh = jnp.where(head0 == is_a, q, 0.0).astype(BF16)
                s_prev = lax.dot_general(qh, k_prev, contract_last, preferred_element_type=F32)
                s_cur = lax.dot_general(qh, k_cur, contract_last, preferred_element_type=F32)
                s_prev = jnp.where(valid_prev & has_prev, s_prev - slope * dist_prev, NEG_INF)
                s_cur = jnp.where(valid_cur, s_cur - slope * dist_cur, NEG_INF)
                m = jnp.maximum(jnp.max(s_prev, axis=-1, keepdims=True), jnp.max(s_cur, axis=-1, keepdims=True))
                p_prev = jnp.exp(s_prev - m)
                p_cur = jnp.exp(s_cur - m)
                denom = jnp.sum(p_prev, axis=-1, keepdims=True) + jnp.sum(p_cur, axis=-1, keepdims=True)
                pv = (jnp.dot(p_prev.astype(BF16), v_prev, preferred_element_type=F32)
                      + jnp.dot(p_cur.astype(BF16), v_cur, preferred_element_type=F32))
                outs.append(pv / denom)
                lses.append(m + jnp.log(denom))
            o_scr[cur, :] = jnp.where(head0, outs[0], outs[1])
            l_scr[cur, :] = jnp.where(head0, lses[0], lses[1])
            return carry

        lax.fori_loop(0, SEQ // QBLK, body, 0)

    def merge(k, carry):
        r = pl.ds(pl.multiple_of(k * QBLK, QBLK), QBLK)
        la, lb, lc = l0[r, :], l1[r, :], l2[r, :]
        m = jnp.maximum(jnp.maximum(la, lb), lc)
        wa, wb, wc = jnp.exp(la - m), jnp.exp(lb - m), jnp.exp(lc - m)
        tot = wa + wb + wc
        o_ref[r, :] = ((wa * o0[r, :] + wb * o1[r, :] + wc * o2[r, :]) / tot).astype(BF16)
        return carry

    lax.fori_loop(0, SEQ // QBLK, merge, 0)


def _attention(proj, slopes):
    pairs = HEADS_PER_PATTERN // 2

    def col(base, g):
        return lambda b, p: (b, (base + g * HEADS_PER_PATTERN * HEAD_DIM) // LANES + p)

    in_specs = [pl.BlockSpec(memory_space=pltpu.SMEM)]
    for g in range(len(DSWA_PATTERNS)):
        for base in (P_Q, P_K, P_V):
            in_specs.append(pl.BlockSpec((SEQ, LANES), col(base, g)))
    scr = pltpu.VMEM((SEQ, LANES), F32)
    return pl.pallas_call(
        _attn_kernel,
        out_shape=jax.ShapeDtypeStruct((TOKENS, ATTN_WIDTH), BF16),
        grid=(BATCH, pairs),
        in_specs=in_specs,
        out_specs=pl.BlockSpec((SEQ, LANES), lambda b, p: (b, p)),
        scratch_shapes=[scr] * 6,
        compiler_params=_cparams(("arbitrary", "arbitrary")),
        name="dilated_attention",
    )(slopes, *([proj] * 9))


def _causal_conv3(z_scr, w_ref, lead, n_rows):
    return (w_ref[0:1, :] * z_scr[pl.ds(lead, n_rows), :]
            + w_ref[1:2, :] * z_scr[pl.ds(lead - 1, n_rows), :]
            + w_ref[2:3, :] * z_scr[pl.ds(lead - 2, n_rows), :])


def _merge_kernel(s_ref, a_ref, cb_ref, cc_ref, ch_ref, cch_ref, chh_ref, g0_ref, g1_ref, g2_ref,
                  x_ref, gt_ref, gpost_ref, cw_ref, wss_ref, wat_ref, wcv_ref, wo_ref,
                  o_ref, z_scr, m_scr):
    i = pl.program_id(0)
    tiles_per_seq = SEQ // TM_MERGE
    b = i // tiles_per_seq
    seq_start = (i % tiles_per_seq) == 0

    halo = cch_ref[...] * chh_ref[...]
    z_scr[0:8, :] = jnp.where(seq_start, 0.0, halo)
    z_scr[8:8 + TM_MERGE, :] = cc_ref[...] * ch_ref[...]
    cv = (cb_ref[...] * _causal_conv3(z_scr, cw_ref, 8, TM_MERGE)).astype(BF16)

    s = s_ref[...]
    a = a_ref[...]
    nb = 512
    for c in range(D_MODEL // nb):
        cols = slice(c * nb, (c + 1) * nb)
        y_ssm = jnp.dot(s, wss_ref[:, cols], preferred_element_type=F32)
        y_att = jnp.dot(a, wat_ref[:, cols], preferred_element_type=F32)
        y_cv = jnp.dot(cv, wcv_ref[:, cols], preferred_element_type=F32)
        merged = g0_ref[:, cols] * y_ssm + g1_ref[:, cols] * y_att + g2_ref[:, cols] * y_cv
        m_scr[:, cols] = merged.astype(BF16)

    y = jnp.dot(m_scr[...], wo_ref[...], preferred_element_type=F32)
    ms = jnp.mean(y * y, axis=-1, keepdims=True)
    yn = y * lax.rsqrt(ms + RMS_EPS) * gpost_ref[...]
    o_ref[...] = x_ref[...] + gt_ref[pl.ds(b, 1), :] * yn


def _merge(s, a, proj, x, mod_l, g_post, conv_w, w_ssm_out, w_attn_out, w_conv_out, w_o):
    tm = TM_MERGE
    d = D_MODEL
    cw = CONV_WIDTH
    cb_blk = P_CONV // cw
    halo_blocks = tm // 8

    def halo_map(col_blk):
        return lambda i: (jnp.maximum(i * halo_blocks - 1, 0), col_blk)

    return pl.pallas_call(
        _merge_kernel,
        out_shape=jax.ShapeDtypeStruct((TOKENS, d), F32),
        grid=(TOKENS // tm,),
        in_specs=[
            pl.BlockSpec((tm, SSM_WIDTH), lambda i: (i, 0)),
            pl.BlockSpec((tm, ATTN_WIDTH), lambda i: (i, 0)),
            pl.BlockSpec((tm, cw), lambda i: (i, cb_blk)),
            pl.BlockSpec((tm, cw), lambda i: (i, cb_blk + 1)),
            pl.BlockSpec((tm, cw), lambda i: (i, cb_blk + 2)),
            pl.BlockSpec((8, cw), halo_map(cb_blk + 1)),
            pl.BlockSpec((8, cw), halo_map(cb_blk + 2)),
            pl.BlockSpec((tm, d), lambda i: (i, 0)),
            pl.BlockSpec((tm, d), lambda i: (i, 1)),
            pl.BlockSpec((tm, d), lambda i: (i, 2)),
            pl.BlockSpec((tm, d), lambda i: (i, 0)),
            pl.BlockSpec((BATCH, d), lambda i: (0, 2)),
            _resident((1, d), lambda i: (0, 0)),
            _resident((3, cw), lambda i: (0, 0)),
            _resident((SSM_WIDTH, d), lambda i: (0, 0)),
            _resident((ATTN_WIDTH, d), lambda i: (0, 0)),
            _resident((cw, d), lambda i: (0, 0)),
            _resident((d, d), lambda i: (0, 0)),
        ],
        out_specs=pl.BlockSpec((tm, d), lambda i: (i, 0)),
        scratch_shapes=[pltpu.VMEM((tm + 8, cw), F32), pltpu.VMEM((tm, d), BF16)],
        compiler_params=_cparams(("arbitrary",)),
        name="branch_merge",
    )(s, a, proj, proj, proj, proj, proj, proj, proj, proj, x, mod_l, g_post, conv_w,
      w_ssm_out, w_attn_out, w_conv_out, w_o)


def _up_kernel(x_ref, xh_ref, g_ref, sh_ref, sc_ref, wa_ref, wb_ref, cwa_ref, cwb_ref, o_ref,
               h_scr, ua_scr, ub_scr):
    i = pl.program_id(0)
    j = pl.program_id(1)
    tiles_per_seq = SEQ // TM_PROJ
    seq_start = (i % tiles_per_seq) == 0

    @pl.when(j == 0)
    def _():
        b = i // tiles_per_seq
        _norm_mod_rows(xh_ref, g_ref, sh_ref, sc_ref, b, h_scr, 0, HALO)
        _norm_mod_rows(x_ref, g_ref, sh_ref, sc_ref, b, h_scr, HALO, TM_PROJ)

    h = h_scr[...]
    for w_ref, u_scr in ((wa_ref, ua_scr), (wb_ref, ub_scr)):
        up = jnp.dot(h, w_ref[...], preferred_element_type=F32)
        u_scr[HALO:, :] = up[HALO:, :]
        u_scr[0:HALO, :] = jnp.where(seq_start, 0.0, up[0:HALO, :])
    ca = _causal_conv3(ua_scr, cwa_ref, HALO, TM_PROJ)
    cb = _causal_conv3(ub_scr, cwb_ref, HALO, TM_PROJ)
    o_ref[...] = (ca * jax.nn.sigmoid(ca) * cb).astype(BF16)


def _mlp_up(x, g, mod_l, w_up, conv_w):
    d = D_MODEL
    tm = TM_PROJ
    tf = TF_UP
    nf = D_FF // tf
    halo_blocks = tm // HALO
    return pl.pallas_call(
        _up_kernel,
        out_shape=jax.ShapeDtypeStruct((TOKENS, D_FF), BF16),
        grid=(TOKENS // tm, nf),
        in_specs=[
            pl.BlockSpec((tm, d), lambda i, j: (i, 0)),
            pl.BlockSpec((HALO, d), lambda i, j: (jnp.maximum(i * halo_blocks - 1, 0), 0)),
            pl.BlockSpec((1, d), lambda i, j: (0, 0)),
            pl.BlockSpec((BATCH, d), lambda i, j: (0, 3)),
            pl.BlockSpec((BATCH, d), lambda i, j: (0, 4)),
            pl.BlockSpec((d, tf), lambda i, j: (0, j)),
            pl.BlockSpec((d, tf), lambda i, j: (0, nf + j)),
            pl.BlockSpec((3, tf), lambda i, j: (0, j)),
            pl.BlockSpec((3, tf), lambda i, j: (0, nf + j)),
        ],
        out_specs=pl.BlockSpec((tm, tf), lambda i, j: (i, j)),
        scratch_shapes=[pltpu.VMEM((tm + HALO, d), BF16),
                        pltpu.VMEM((tm + HALO, tf), F32),
                        pltpu.VMEM((tm + HALO, tf), F32)],
        compiler_params=_cparams(("arbitrary", "arbitrary")),
        name="mlp_up",
    )(x, x, g, mod_l, mod_l, w_up, w_up, conv_w, conv_w)


def _down_kernel(act_ref, w_ref, x_ref, gt_ref, gpost_ref, o_ref):
    b = pl.program_id(0) // (SEQ // TM_DOWN)
    y = jnp.dot(act_ref[...], w_ref[...], preferred_element_type=F32)
    ms = jnp.mean(y * y, axis=-1, keepdims=True)
    yn = y * lax.rsqrt(ms + RMS_EPS) * gpost_ref[...]
    o_ref[...] = x_ref[...] + gt_ref[pl.ds(b, 1), :] * yn


def _mlp_down(act, w_down, x, mod_l, g_post):
    d = D_MODEL
    tm = TM_DOWN
    return pl.pallas_call(
        _down_kernel,
        out_shape=jax.ShapeDtypeStruct((TOKENS, d), F32),
        grid=(TOKENS // tm,),
        in_specs=[
            pl.BlockSpec((tm, D_FF), lambda i: (i, 0)),
            _resident((D_FF, d), lambda i: (0, 0)),
            pl.BlockSpec((tm, d), lambda i: (i, 0)),
            pl.BlockSpec((BATCH, d), lambda i: (0, 5)),
            _resident((1, d), lambda i: (0, 0)),
        ],
        out_specs=pl.BlockSpec((tm, d), lambda i: (i, 0)),
        compiler_params=_cparams(("arbitrary",)),
        name="mlp_down",
    )(act, w_down, x, mod_l, g_post)


def _alibi_slopes():
    h = N_ATTN_HEADS
    return np.array([2.0 ** (-8.0 * (i + 1) / h) for i in range(h)], dtype=np.float32)


def _time_major_permutation():
    rows = BATCH * SSM_CHUNK
    i = np.arange(rows)
    src = (i % BATCH) * SSM_CHUNK + i // BATCH
    p = np.zeros((rows, rows), np.float32)
    p[i, src] = 1.0
    return p


def kernel(x, c, w_mod, b_mod, g_pre_mix, g_post_mix, g_pre_ffn, g_post_ffn, w_in, ssm_log_dt, ssm_a_re,
           ssm_a_im, ssm_b_re, ssm_b_im, ssm_c_re, ssm_c_im, ssm_d, w_glu, b_glu, conv_mix_w, w_ssm_out,
           w_attn_out, w_conv_out, b_gate, w_o, w_up, ffn_conv_w, w_down):
    off_q, off_conv, off_gate = SSM_WIDTH, SSM_WIDTH + 3 * QKV_WIDTH, SSM_WIDTH + 3 * QKV_WIDTH + 3 * CONV_WIDTH
    w_in_p = jnp.concatenate([w_in[..., off_gate:], w_in[..., :off_q], w_in[..., off_conv:off_gate],
                              w_in[..., off_q:off_conv]], axis=-1).astype(BF16)
    bias_p = jnp.concatenate([b_gate, jnp.zeros((DEPTH, N_IN - N_BRANCH * D_MODEL), F32)], axis=-1)
    w_glu_b = w_glu.astype(BF16)
    w_ssm_out_b = w_ssm_out.astype(BF16)
    w_attn_out_b = w_attn_out.astype(BF16)
    w_conv_out_b = w_conv_out.astype(BF16)
    w_o_b = w_o.astype(BF16)
    w_up_b = w_up.astype(BF16)
    w_down_b = w_down.astype(BF16)

    mod = _modulation(c, w_mod, b_mod)
    lr, li, bbr, bbi = _ssm_params(ssm_log_dt, ssm_a_re, ssm_a_im, ssm_b_re, ssm_b_im)
    perm_np = _time_major_permutation()
    perm = jnp.asarray(perm_np, dtype=BF16)
    permt = jnp.asarray(perm_np.T, dtype=BF16)
    slopes = jnp.asarray(_alibi_slopes())

    xf = x.reshape(TOKENS, D_MODEL)
    for l in range(DEPTH):
        mod_l = mod[l]
        proj = _in_projection(xf, g_pre_mix[l][None], mod_l, w_in_p[l], bias_p[l][None])
        bb, cre, cim, lam = _ssm_layout(lr[l], li[l], bbr[:, l], bbi[:, l], ssm_c_re[l], ssm_c_im[l])
        s = _ssm_mixer(proj.reshape(BATCH, SEQ, N_IN), perm, permt, bb, cre, cim, lam,
                       ssm_d[l][None], w_glu_b[l], b_glu[l][None])
        a = _attention(proj, slopes)
        xf = _merge(s.reshape(TOKENS, SSM_WIDTH), a, proj, xf, mod_l, g_post_mix[l][None], conv_mix_w[l],
                    w_ssm_out_b[l], w_attn_out_b[l], w_conv_out_b[l], w_o_b[l])
        act = _mlp_up(xf, g_pre_ffn[l][None], mod_l, w_up_b[l], ffn_conv_w[l])
        xf = _mlp_down(act, w_down_b[l], xf, mod_l, g_post_ffn[l][None])
    return xf.reshape(BATCH, SEQ, D_MODEL)
```

```python
import numpy as np
import jax
import jax.numpy as jnp
from jax import lax
from jax.experimental import pallas as pl
from jax.experimental.pallas import tpu as pltpu

F32 = jnp.float32
BF16 = jnp.bfloat16

D_MODEL = 2048
BATCH = 8
SEQ = 2048
DEPTH = 4
TOKENS = BATCH * SEQ
RMS_EPS = 1e-6
NEG_INF = -1e30
N_BRANCH = 3
SSM_WIDTH = 512
SSM_GROUP = 16
SSM_GROUPS = 32
SSM_STATE = 64
HEAD_DIM = 64
DSWA_PATTERNS = ((128, 1), (512, 4), (2048, 16))
ATTN_WIDTH = 512
HEADS_PER_PATTERN = 8
N_ATTN_HEADS = 24
QKV_WIDTH = 1536
CONV_WIDTH = 512
D_FF = 5632
N_IN = 12800

P_GATE = 0
P_U = N_BRANCH * D_MODEL
P_CONV = P_U + SSM_WIDTH
P_Q = P_CONV + 3 * CONV_WIDTH
P_K = P_Q + QKV_WIDTH
P_V = P_K + QKV_WIDTH

LANES = 128
MXU_COLS = 256
VMEM_LIMIT = 56 * 1024 * 1024

TM_PROJ = 1024
TN_PROJ = 1280
TF_UP = 512
HALO = 16
TM_MERGE = 256
TM_DOWN = 256
SSM_CHUNK = 128
QBLK = 128


def _cparams(sem):
    return pltpu.CompilerParams(dimension_semantics=sem, vmem_limit_bytes=VMEM_LIMIT)


def _resident(shape, index_map):
    return pl.BlockSpec(shape, index_map, pipeline_mode=pl.Buffered(1))


def _sigmoid(x):
    return 0.5 * jnp.tanh(0.5 * x) + 0.5


def _rows3(a):
    return a.reshape(DEPTH, 1, a.shape[-1])


def _mod_kernel(c_ref, w_ref, b_ref, o_ref):
    c = c_ref[...]
    cond = (c * jax.nn.sigmoid(c)).astype(BF16)
    o_ref[0] = jnp.dot(cond, w_ref[0].astype(BF16), preferred_element_type=F32) + b_ref[0]


def _modulation(c, w_mod, b_mod):
    tn = 1024
    n = 6 * D_MODEL
    return pl.pallas_call(
        _mod_kernel,
        out_shape=jax.ShapeDtypeStruct((DEPTH, BATCH, n), F32),
        grid=(DEPTH, n // tn),
        in_specs=[
            pl.BlockSpec((BATCH, D_MODEL), lambda l, j: (0, 0)),
            pl.BlockSpec((1, D_MODEL, tn), lambda l, j: (l, 0, j)),
            pl.BlockSpec((1, 1, tn), lambda l, j: (l, 0, j)),
        ],
        out_specs=pl.BlockSpec((1, BATCH, tn), lambda l, j: (l, 0, j)),
        compiler_params=_cparams(("arbitrary", "arbitrary")),
        name="modulation",
    )(c, w_mod, b_mod.reshape(DEPTH, 1, n))


def _norm_mod_rows(x_ref, g_ref, sh_ref, sc_ref, b, h_scr, dst_row, n_rows):
    rows = 128 if n_rows % 128 == 0 else n_rows
    g = g_ref[...]
    scale = 1.0 + sc_ref[pl.ds(b, 1), :]
    shift = sh_ref[pl.ds(b, 1), :]

    def body(k, carry):
        r0 = pl.multiple_of(k * rows, rows)
        x = x_ref[pl.ds(r0, rows), :]
        ms = jnp.mean(x * x, axis=-1, keepdims=True)
        y = x * lax.rsqrt(ms + RMS_EPS) * g
        h_scr[pl.ds(dst_row + r0, rows), :] = (y * scale + shift).astype(BF16)
        return carry

    lax.fori_loop(0, n_rows // rows, body, 0)


def _inproj_kernel(x_ref, g_ref, sh_ref, sc_ref, w_ref, bias_ref, o_ref, h_scr):
    i = pl.program_id(0)
    j = pl.program_id(1)

    @pl.when(j == 0)
    def _():
        _norm_mod_rows(x_ref, g_ref, sh_ref, sc_ref, i // (SEQ // TM_PROJ), h_scr, 0, TM_PROJ)

    full_gate_tiles = P_U // TN_PROJ
    gate_cols_in_mixed = P_U - full_gate_tiles * TN_PROJ

    @pl.when(j < full_gate_tiles)
    def _():
        acc = jnp.dot(h_scr[...], w_ref[...], preferred_element_type=F32)
        o_ref[...] = _sigmoid(acc + bias_ref[...])

    @pl.when(j == full_gate_tiles)
    def _():
        acc = jnp.dot(h_scr[...], w_ref[...], preferred_element_type=F32)
        col = lax.broadcasted_iota(jnp.int32, acc.shape, 1)
        o_ref[...] = jnp.where(col < gate_cols_in_mixed, _sigmoid(acc + bias_ref[...]), acc)

    @pl.when(j > full_gate_tiles)
    def _():
        o_ref[...] = jnp.dot(h_scr[...], w_ref[...], preferred_element_type=F32)


def _in_projection(l, x, g3, mod, w_in_p, bias_p3):
    d = D_MODEL
    return pl.pallas_call(
        _inproj_kernel,
        out_shape=jax.ShapeDtypeStruct((TOKENS, N_IN), F32),
        grid=(TOKENS // TM_PROJ, N_IN // TN_PROJ),
        in_specs=[
            pl.BlockSpec((TM_PROJ, d), lambda i, j: (i, 0)),
            pl.BlockSpec((None, 1, d), lambda i, j: (l, 0, 0)),
            pl.BlockSpec((None, BATCH, d), lambda i, j: (l, 0, 0)),
            pl.BlockSpec((None, BATCH, d), lambda i, j: (l, 0, 1)),
            pl.BlockSpec((None, d, TN_PROJ), lambda i, j: (l, 0, j)),
            pl.BlockSpec((None, 1, TN_PROJ), lambda i, j: (l, 0, j)),
        ],
        out_specs=pl.BlockSpec((TM_PROJ, TN_PROJ), lambda i, j: (i, j)),
        scratch_shapes=[pltpu.VMEM((TM_PROJ, d), BF16)],
        compiler_params=_cparams(("arbitrary", "arbitrary")),
        name="in_projection",
    )(x, g3, mod, mod, w_in_p, bias_p3)


def _ssm_param_kernel(ldt_ref, ar_ref, ai_ref, br_ref, bi_ref, lr_ref, li_ref, bbr_ref, bbi_ref):
    dt = jnp.exp(ldt_ref[...])
    ar = ar_ref[...]
    ai = ai_ref[...]
    mag = jnp.exp(ar * dt)
    lr = mag * jnp.cos(ai * dt)
    li = mag * jnp.sin(ai * dt)
    den = ar * ar + ai * ai
    fr = ((lr - 1.0) * ar + li * ai) / den
    fi = (li * ar - (lr - 1.0) * ai) / den
    lr_ref[...] = lr
    li_ref[...] = li
    for ch in range(SSM_GROUP):
        br = br_ref[ch]
        bi = bi_ref[ch]
        bbr_ref[ch] = fr * br - fi * bi
        bbi_ref[ch] = fr * bi + fi * br


def _ssm_params(log_dt, a_re, a_im, b_re, b_im):
    n = DEPTH * SSM_GROUPS
    mat = jax.ShapeDtypeStruct((n, SSM_STATE), F32)
    cube = jax.ShapeDtypeStruct((SSM_GROUP, n, SSM_STATE), F32)

    def channel_major(b):
        return b.transpose(3, 0, 1, 2).reshape(SSM_GROUP, n, SSM_STATE)

    lr, li, bbr, bbi = pl.pallas_call(
        _ssm_param_kernel,
        out_shape=(mat, mat, cube, cube),
        name="ssm_params",
    )(log_dt.reshape(n, 1), a_re.reshape(n, SSM_STATE), a_im.reshape(n, SSM_STATE),
      channel_major(b_re), channel_major(b_im))
    shape3 = (DEPTH, SSM_GROUPS, SSM_STATE)
    return (lr.reshape(shape3), li.reshape(shape3),
            bbr.reshape((SSM_GROUP,) + shape3), bbi.reshape((SSM_GROUP,) + shape3))


N_SLAB = SSM_WIDTH // LANES
GROUPS_PER_SLAB = LANES // SSM_GROUP
STATES_PER_SLAB = GROUPS_PER_SLAB * SSM_STATE
N_STATES = SSM_GROUPS * SSM_STATE


def _ssm_layout(lr, li, bbr, bbi, c_re, c_im):
    eye = jnp.eye(GROUPS_PER_SLAB, dtype=F32)

    def in_blocks(bb):
        t = bb.reshape(SSM_GROUP, N_SLAB, GROUPS_PER_SLAB, SSM_STATE).transpose(1, 2, 0, 3)
        blk = eye[None, :, None, :, None] * t[:, :, :, None, :]
        return blk.reshape(N_SLAB, LANES, STATES_PER_SLAB)

    def out_blocks(cc):
        t = cc.reshape(N_SLAB, GROUPS_PER_SLAB, SSM_GROUP, SSM_STATE).transpose(0, 1, 3, 2)
        blk = eye[None, :, None, :, None] * t[:, :, :, None, :]
        return blk.reshape(N_SLAB, STATES_PER_SLAB, LANES)

    bb = jnp.concatenate([in_blocks(bbr), in_blocks(bbi)], axis=-1).astype(BF16)
    lam = jnp.stack([lr.reshape(N_STATES), li.reshape(N_STATES)], axis=0)
    return bb, out_blocks(c_re).astype(BF16), out_blocks(c_im).astype(BF16), lam


def _ssm_kernel(u_ref, perm_ref, permt_ref, bb_ref, cre_ref, cim_ref, lam_ref, dsk_ref, wglu_ref, bglu_ref,
                o_ref, x_scr, h_scr):
    rows = BATCH * SSM_CHUNK

    @pl.when(pl.program_id(0) == 0)
    def _():
        h_scr[...] = jnp.zeros_like(h_scr)

    u = u_ref[...].reshape(rows, SSM_WIDTH).astype(BF16)
    u_tm = jnp.dot(perm_ref[...], u, preferred_element_type=F32)
    u_tm_b = u_tm.astype(BF16)

    for s in range(N_SLAB):
        xs = jnp.dot(u_tm_b[:, s * LANES:(s + 1) * LANES], bb_ref[s], preferred_element_type=F32)
        x_scr[:, s * STATES_PER_SLAB:(s + 1) * STATES_PER_SLAB] = xs[:, :STATES_PER_SLAB]
        x_scr[:, N_STATES + s * STATES_PER_SLAB:N_STATES + (s + 1) * STATES_PER_SLAB] = xs[:, STATES_PER_SLAB:]

    width = STATES_PER_SLAB
    for s in range(N_STATES // width):
        re_cols = pl.ds(s * width, width)
        im_cols = pl.ds(N_STATES + s * width, width)
        lr = jnp.broadcast_to(lam_ref[0:1, s * width:(s + 1) * width], (BATCH, width))
        li = jnp.broadcast_to(lam_ref[1:2, s * width:(s + 1) * width], (BATCH, width))

        def step(t, carry, re_cols=re_cols, im_cols=im_cols, lr=lr, li=li):
            hr, hi = carry
            r0 = pl.multiple_of(t * BATCH, BATCH)
            xr = x_scr[pl.ds(r0, BATCH), re_cols]
            xi = x_scr[pl.ds(r0, BATCH), im_cols]
            nr = lr * hr - li * hi + xr
            ni = lr * hi + li * hr + xi
            x_scr[pl.ds(r0, BATCH), re_cols] = nr
            x_scr[pl.ds(r0, BATCH), im_cols] = ni
            return nr, ni

        hr, hi = lax.fori_loop(0, SSM_CHUNK, step, (h_scr[:, re_cols], h_scr[:, im_cols]), unroll=4)
        h_scr[:, re_cols] = hr
        h_scr[:, im_cols] = hi

    ys = []
    for s in range(N_SLAB):
        h_re = x_scr[:, s * STATES_PER_SLAB:(s + 1) * STATES_PER_SLAB].astype(BF16)
        h_im = x_scr[:, N_STATES + s * STATES_PER_SLAB:N_STATES + (s + 1) * STATES_PER_SLAB].astype(BF16)
        ys.append(jnp.dot(h_re, cre_ref[s], preferred_element_type=F32)
                  - jnp.dot(h_im, cim_ref[s], preferred_element_type=F32))
    y = jnp.concatenate(ys, axis=-1) + dsk_ref[...] * u_tm
    g = jax.nn.gelu(y)
    gate = jnp.dot(g.astype(BF16), wglu_ref[...], preferred_element_type=F32) + bglu_ref[...]
    out_tm = (g * _sigmoid(gate)).astype(BF16)
    out = jnp.dot(permt_ref[...], out_tm, preferred_element_type=F32)
    o_ref[...] = out.astype(BF16).reshape(BATCH, SSM_CHUNK, SSM_WIDTH)


def _ssm_mixer(l, proj3, perm, permt, bb, cre, cim, lam, d_skip3, w_glu, b_glu3):
    rows = BATCH * SSM_CHUNK
    return pl.pallas_call(
        _ssm_kernel,
        out_shape=jax.ShapeDtypeStruct((BATCH, SEQ, SSM_WIDTH), BF16),
        grid=(SEQ // SSM_CHUNK,),
        in_specs=[
            pl.BlockSpec((BATCH, SSM_CHUNK, SSM_WIDTH), lambda c: (0, c, P_U // SSM_WIDTH)),
            _resident((rows, rows), lambda c: (0, 0)),
            _resident((rows, rows), lambda c: (0, 0)),
            _resident((N_SLAB, LANES, 2 * STATES_PER_SLAB), lambda c: (0, 0, 0)),
            _resident((N_SLAB, STATES_PER_SLAB, LANES), lambda c: (0, 0, 0)),
            _resident((N_SLAB, STATES_PER_SLAB, LANES), lambda c: (0, 0, 0)),
            _resident((2, N_STATES), lambda c: (0, 0)),
            _resident((None, 1, SSM_WIDTH), lambda c: (l, 0, 0)),
            _resident((None, SSM_WIDTH, SSM_WIDTH), lambda c: (l, 0, 0)),
            _resident((None, 1, SSM_WIDTH), lambda c: (l, 0, 0)),
        ],
        out_specs=pl.BlockSpec((BATCH, SSM_CHUNK, SSM_WIDTH), lambda c: (0, c, 0)),
        scratch_shapes=[pltpu.VMEM((rows, 2 * N_STATES), F32), pltpu.VMEM((BATCH, 2 * N_STATES), F32)],
        compiler_params=_cparams(("arbitrary",)),
        name="ssm_mixer",
    )(proj3, perm, permt, bb, cre, cim, lam, d_skip3, w_glu, b_glu3)


def _attn_kernel(slopes_ref, q0, k0, v0, q1, k1, v1, q2, k2, v2, o_ref,
                 qp, kt, va, vb, o0, l0, o1, l1, o2, l2):
    pair = pl.program_id(1)
    n_qblk = SEQ // QBLK
    lane = lax.broadcasted_iota(jnp.int32, (QBLK, LANES), 1)
    head0 = lane < HEAD_DIM
    qi = lax.broadcasted_iota(jnp.int32, (QBLK, 2 * QBLK), 0)
    kj = lax.broadcasted_iota(jnp.int32, (QBLK, 2 * QBLK), 1)
    dist = QBLK + qi - kj
    valid = (dist >= 0) & (dist <= QBLK)
    dist = dist.astype(F32)

    groups = ((q0, k0, v0, o0, l0), (q1, k1, v1, o1, l1), (q2, k2, v2, o2, l2))
    for g, (_, dil) in enumerate(DSWA_PATTERNS):
        q_ref, k_ref, v_ref, o_scr, l_scr = groups[g]
        n_blocks = SEQ // (dil * QBLK)

        def rows(start, dil=dil):
            if dil == 1:
                return pl.ds(pl.multiple_of(start, QBLK), QBLK)
            return pl.ds(start, QBLK, stride=dil)

        def natural_rows(idx, rows=rows, n_blocks=n_blocks, dil=dil):
            return rows(dil * QBLK * (idx % n_blocks) + idx // n_blocks)

        def reorder(idx, carry, q_ref=q_ref, k_ref=k_ref, v_ref=v_ref, natural_rows=natural_rows):
            src = natural_rows(idx)
            dst = pl.ds(pl.multiple_of(idx * QBLK, QBLK), QBLK)
            qp[dst, :] = (q_ref[src, :] * (HEAD_DIM ** -0.5)).astype(BF16)
            kt[idx] = k_ref[src, :].T.astype(BF16)
            v = v_ref[src, :]
            va[dst, :] = jnp.where(head0, v, 1.0).astype(BF16)
            vb[dst, :] = jnp.where(head0, 1.0, v).astype(BF16)
            return carry

        lax.fori_loop(0, n_qblk, reorder, 0, unroll=2)

        bias = []
        for hh in range(2):
            slope = slopes_ref[g * HEADS_PER_PATTERN + 2 * pair + hh] * float(dil)
            bias.append(jnp.where(valid, -slope * dist, NEG_INF))

        def block(idx, with_prev, o_scr=o_scr, l_scr=l_scr, bias=bias, natural_rows=natural_rows):
            cur = pl.ds(pl.multiple_of(idx * QBLK, QBLK), QBLK)
            q = qp[cur, :]
            if with_prev:
                keys_t = jnp.concatenate([kt[idx - 1], kt[idx]], axis=1)
                kv_rows = pl.ds(pl.multiple_of((idx - 1) * QBLK, QBLK), 2 * QBLK)
            else:
                keys_t = kt[idx]
                kv_rows = cur
            pvs, ms = [], []
            for hh, v_scr in ((0, va), (1, vb)):
                qh = jnp.where(head0 == (hh == 0), q, jnp.zeros_like(q))
                s = jnp.dot(qh, keys_t, preferred_element_type=F32)
                s = s + (bias[hh] if with_prev else bias[hh][:, QBLK:])
                m = jnp.max(s, axis=-1, keepdims=True)
                p = jnp.exp(s - m).astype(BF16)
                pvs.append(jnp.dot(p, v_scr[kv_rows, :], preferred_element_type=F32))
                ms.append(m)
            num = jnp.where(head0, pvs[0], pvs[1])
            den = pltpu.roll(jnp.where(head0, pvs[1], pvs[0]), HEAD_DIM, axis=1)
            dst = natural_rows(idx)
            o_scr[dst, :] = num / den
            l_scr[dst, :] = jnp.where(head0, ms[0], ms[1]) + jnp.log(den)

        def first_blocks(res, carry, block=block, n_blocks=n_blocks):
            block(res * n_blocks, False)
            return carry

        lax.fori_loop(0, dil, first_blocks, 0, unroll=min(dil, 4))

        if n_blocks > 1:
            later = n_blocks - 1

            def later_blocks(k, carry, block=block, later=later, n_blocks=n_blocks):
                block((k // later) * n_blocks + k % later + 1, True)
                return carry

            lax.fori_loop(0, dil * later, later_blocks, 0, unroll=3)

    def merge(k, carry):
        r = pl.ds(pl.multiple_of(k * QBLK, QBLK), QBLK)
        la, lb, lc = l0[r, :], l1[r, :], l2[r, :]
        m = jnp.maximum(jnp.maximum(la, lb), lc)
        wa, wb, wc = jnp.exp(la - m), jnp.exp(lb - m), jnp.exp(lc - m)
        tot = wa + wb + wc
        o_ref[r, :] = ((wa * o0[r, :] + wb * o1[r, :] + wc * o2[r, :]) / tot).astype(BF16)
        return carry

    lax.fori_loop(0, n_qblk, merge, 0, unroll=2)


def _attention(proj, slopes):
    pairs = HEADS_PER_PATTERN // 2

    def col(base, g):
        return lambda b, p: (b, (base + g * HEADS_PER_PATTERN * HEAD_DIM) // LANES + p)

    in_specs = [pl.BlockSpec(memory_space=pltpu.SMEM)]
    for g in range(len(DSWA_PATTERNS)):
        for base in (P_Q, P_K, P_V):
            in_specs.append(pl.BlockSpec((SEQ, LANES), col(base, g)))
    scr = pltpu.VMEM((SEQ, LANES), F32)
    seq_b = pltpu.VMEM((SEQ, LANES), BF16)
    kt_b = pltpu.VMEM((SEQ // QBLK, LANES, QBLK), BF16)
    return pl.pallas_call(
        _attn_kernel,
        out_shape=jax.ShapeDtypeStruct((TOKENS, ATTN_WIDTH), BF16),
        grid=(BATCH, pairs),
        in_specs=in_specs,
        out_specs=pl.BlockSpec((SEQ, LANES), lambda b, p: (b, p)),
        scratch_shapes=[seq_b, kt_b, seq_b, seq_b] + [scr] * 6,
        compiler_params=_cparams(("arbitrary", "arbitrary")),
        name="dilated_attention",
    )(slopes, *([proj] * 9))


def _causal_conv3(z_scr, w_ref, lead, n_rows, cols=slice(None)):
    return (w_ref[0:1, cols] * z_scr[pl.ds(lead, n_rows), cols]
            + w_ref[1:2, cols] * z_scr[pl.ds(lead - 1, n_rows), cols]
            + w_ref[2:3, cols] * z_scr[pl.ds(lead - 2, n_rows), cols])


def _merge_kernel(s_ref, a_ref, cb_ref, cc_ref, ch_ref, cch_ref, chh_ref, g0_ref, g1_ref, g2_ref,
                  x_ref, gt_ref, gpost_ref, cw_ref, wss_ref, wat_ref, wcv_ref, wo_ref,
                  o_ref, z_scr, m_scr):
    i = pl.program_id(0)
    tiles_per_seq = SEQ // TM_MERGE
    b = i // tiles_per_seq
    seq_start = (i % tiles_per_seq) == 0

    halo = cch_ref[...] * chh_ref[...]
    z_scr[0:8, :] = jnp.where(seq_start, 0.0, halo)
    z_scr[8:8 + TM_MERGE, :] = cc_ref[...] * ch_ref[...]
    cv = (cb_ref[...] * _causal_conv3(z_scr, cw_ref, 8, TM_MERGE)).astype(BF16)

    s = s_ref[...]
    a = a_ref[...]
    nb = 512
    for c in range(D_MODEL // nb):
        cols = slice(c * nb, (c + 1) * nb)
        y_ssm = jnp.dot(s, wss_ref[:, cols], preferred_element_type=F32)
        y_att = jnp.dot(a, wat_ref[:, cols], preferred_element_type=F32)
        y_cv = jnp.dot(cv, wcv_ref[:, cols], preferred_element_type=F32)
        merged = g0_ref[:, cols] * y_ssm + g1_ref[:, cols] * y_att + g2_ref[:, cols] * y_cv
        m_scr[:, cols] = merged.astype(BF16)

    y = jnp.dot(m_scr[...], wo_ref[...], preferred_element_type=F32)
    ms = jnp.mean(y * y, axis=-1, keepdims=True)
    yn = y * lax.rsqrt(ms + RMS_EPS) * gpost_ref[...]
    o_ref[...] = x_ref[...] + gt_ref[pl.ds(b, 1), :] * yn


def _merge(l, s, a, proj, x, mod, g_post3, conv_w, w_ssm_out, w_attn_out, w_conv_out, w_o):
    tm = TM_MERGE
    d = D_MODEL
    cw = CONV_WIDTH
    cb_blk = P_CONV // cw
    halo_blocks = tm // 8

    def halo_map(col_blk):
        return lambda i: (jnp.maximum(i * halo_blocks - 1, 0), col_blk)

    return pl.pallas_call(
        _merge_kernel,
        out_shape=jax.ShapeDtypeStruct((TOKENS, d), F32),
        grid=(TOKENS // tm,),
        in_specs=[
            pl.BlockSpec((tm, SSM_WIDTH), lambda i: (i, 0)),
            pl.BlockSpec((tm, ATTN_WIDTH), lambda i: (i, 0)),
            pl.BlockSpec((tm, cw), lambda i: (i, cb_blk)),
            pl.BlockSpec((tm, cw), lambda i: (i, cb_blk + 1)),
            pl.BlockSpec((tm, cw), lambda i: (i, cb_blk + 2)),
            pl.BlockSpec((8, cw), halo_map(cb_blk + 1)),
            pl.BlockSpec((8, cw), halo_map(cb_blk + 2)),
            pl.BlockSpec((tm, d), lambda i: (i, 0)),
            pl.BlockSpec((tm, d), lambda i: (i, 1)),
            pl.BlockSpec((tm, d), lambda i: (i, 2)),
            pl.BlockSpec((tm, d), lambda i: (i, 0)),
            pl.BlockSpec((None, BATCH, d), lambda i: (l, 0, 2)),
            _resident((None, 1, d), lambda i: (l, 0, 0)),
            _resident((None, 3, cw), lambda i: (l, 0, 0)),
            _resident((None, SSM_WIDTH, d), lambda i: (l, 0, 0)),
            _resident((None, ATTN_WIDTH, d), lambda i: (l, 0, 0)),
            _resident((None, cw, d), lambda i: (l, 0, 0)),
            _resident((None, d, d), lambda i: (l, 0, 0)),
        ],
        out_specs=pl.BlockSpec((tm, d), lambda i: (i, 0)),
        scratch_shapes=[pltpu.VMEM((tm + 8, cw), F32), pltpu.VMEM((tm, d), BF16)],
        compiler_params=_cparams(("arbitrary",)),
        name="branch_merge",
    )(s, a, proj, proj, proj, proj, proj, proj, proj, proj, x, mod, g_post3, conv_w,
      w_ssm_out, w_attn_out, w_conv_out, w_o)


def _up_kernel(x_ref, xh_ref, g_ref, sh_ref, sc_ref, wa_ref, wb_ref, cwa_ref, cwb_ref, o_ref, h_scr):
    i = pl.program_id(0)
    j = pl.program_id(1)
    tiles_per_seq = SEQ // TM_PROJ
    seq_start = (i % tiles_per_seq) == 0

    @pl.when(j == 0)
    def _():
        b = i // tiles_per_seq
        _norm_mod_rows(xh_ref, g_ref, sh_ref, sc_ref, b, h_scr, 0, HALO)
        _norm_mod_rows(x_ref, g_ref, sh_ref, sc_ref, b, h_scr, HALO, TM_PROJ)

    h = h_scr[...]
    row = lax.broadcasted_iota(jnp.int32, (TM_PROJ + HALO, 1), 0)
    keep = jnp.logical_or(row >= HALO, jnp.logical_not(seq_start))
    convs = []
    for w_ref, cw_ref in ((wa_ref, cwa_ref), (wb_ref, cwb_ref)):
        z = jnp.where(keep, jnp.dot(h, w_ref[...], preferred_element_type=F32), 0.0)
        y = (cw_ref[0:1, :] * z + cw_ref[1:2, :] * pltpu.roll(z, 1, axis=0)
             + cw_ref[2:3, :] * pltpu.roll(z, 2, axis=0))
        convs.append(y[HALO:, :])
    ca, cb = convs
    o_ref[...] = (ca * _sigmoid(ca) * cb).astype(BF16)


def _mlp_up(l, x, g3, mod, w_up, conv_w):
    d = D_MODEL
    tm = TM_PROJ
    tf = TF_UP
    nf = D_FF // tf
    halo_blocks = tm // HALO
    return pl.pallas_call(
        _up_kernel,
        out_shape=jax.ShapeDtypeStruct((TOKENS, D_FF), BF16),
        grid=(TOKENS // tm, nf),
        in_specs=[
            pl.BlockSpec((tm, d), lambda i, j: (i, 0)),
            pl.BlockSpec((HALO, d), lambda i, j: (jnp.maximum(i * halo_blocks - 1, 0), 0)),
            pl.BlockSpec((None, 1, d), lambda i, j: (l, 0, 0)),
            pl.BlockSpec((None, BATCH, d), lambda i, j: (l, 0, 3)),
            pl.BlockSpec((None, BATCH, d), lambda i, j: (l, 0, 4)),
            pl.BlockSpec((None, d, tf), lambda i, j: (l, 0, j)),
            pl.BlockSpec((None, d, tf), lambda i, j: (l, 0, nf + j)),
            pl.BlockSpec((None, 3, tf), lambda i, j: (l, 0, j)),
            pl.BlockSpec((None, 3, tf), lambda i, j: (l, 0, nf + j)),
        ],
        out_specs=pl.BlockSpec((tm, tf), lambda i, j: (i, j)),
        scratch_shapes=[pltpu.VMEM((tm + HALO, d), BF16)],
        compiler_params=_cparams(("arbitrary", "arbitrary")),
        name="mlp_up",
    )(x, x, g3, mod, mod, w_up, w_up, conv_w, conv_w)


def _down_kernel(act_ref, w_ref, x_ref, gt_ref, gpost_ref, o_ref):
    b = pl.program_id(0) // (SEQ // TM_DOWN)
    y = jnp.dot(act_ref[...], w_ref[...], preferred_element_type=F32)
    ms = jnp.mean(y * y, axis=-1, keepdims=True)
    yn = y * lax.rsqrt(ms + RMS_EPS) * gpost_ref[...]
    o_ref[...] = x_ref[...] + gt_ref[pl.ds(b, 1), :] * yn


def _mlp_down(l, act, w_down, x, mod, g_post3):
    d = D_MODEL
    tm = TM_DOWN
    return pl.pallas_call(
        _down_kernel,
        out_shape=jax.ShapeDtypeStruct((TOKENS, d), F32),
        grid=(TOKENS // tm,),
        in_specs=[
            pl.BlockSpec((tm, D_FF), lambda i: (i, 0)),
            _resident((None, D_FF, d), lambda i: (l, 0, 0)),
            pl.BlockSpec((tm, d), lambda i: (i, 0)),
            pl.BlockSpec((None, BATCH, d), lambda i: (l, 0, 5)),
            _resident((None, 1, d), lambda i: (l, 0, 0)),
        ],
        out_specs=pl.BlockSpec((tm, d), lambda i: (i, 0)),
        compiler_params=_cparams(("arbitrary",)),
        name="mlp_down",
    )(act, w_down, x, mod, g_post3)


def _alibi_slopes():
    h = N_ATTN_HEADS
    return np.array([2.0 ** (-8.0 * (i + 1) / h) for i in range(h)], dtype=np.float32)


def _time_major_permutation():
    rows = BATCH * SSM_CHUNK
    i = np.arange(rows)
    src = (i % BATCH) * SSM_CHUNK + i // BATCH
    p = np.zeros((rows, rows), np.float32)
    p[i, src] = 1.0
    return p


def kernel(x, c, w_mod, b_mod, g_pre_mix, g_post_mix, g_pre_ffn, g_post_ffn, w_in, ssm_log_dt, ssm_a_re,
           ssm_a_im, ssm_b_re, ssm_b_im, ssm_c_re, ssm_c_im, ssm_d, w_glu, b_glu, conv_mix_w, w_ssm_out,
           w_attn_out, w_conv_out, b_gate, w_o, w_up, ffn_conv_w, w_down):
    off_q, off_conv, off_gate = SSM_WIDTH, SSM_WIDTH + 3 * QKV_WIDTH, SSM_WIDTH + 3 * QKV_WIDTH + 3 * CONV_WIDTH
    w_in_p = jnp.concatenate([w_in[..., off_gate:], w_in[..., :off_q], w_in[..., off_conv:off_gate],
                              w_in[..., off_q:off_conv]], axis=-1).astype(BF16)
    bias_p3 = _rows3(jnp.concatenate([b_gate, jnp.zeros((DEPTH, N_IN - N_BRANCH * D_MODEL), F32)], axis=-1))
    w_glu_b = w_glu.astype(BF16)
    w_ssm_out_b = w_ssm_out.astype(BF16)
    w_attn_out_b = w_attn_out.astype(BF16)
    w_conv_out_b = w_conv_out.astype(BF16)
    w_o_b = w_o.astype(BF16)
    w_up_b = w_up.astype(BF16)
    w_down_b = w_down.astype(BF16)
    g_pre_mix3, g_post_mix3 = _rows3(g_pre_mix), _rows3(g_post_mix)
    g_pre_ffn3, g_post_ffn3 = _rows3(g_pre_ffn), _rows3(g_post_ffn)
    d_skip3, b_glu3 = _rows3(ssm_d), _rows3(b_glu)

    mod = _modulation(c, w_mod, b_mod)
    lr, li, bbr, bbi = _ssm_params(ssm_log_dt, ssm_a_re, ssm_a_im, ssm_b_re, ssm_b_im)
    perm_np = _time_major_permutation()
    perm = jnp.asarray(perm_np, dtype=BF16)
    permt = jnp.asarray(perm_np.T, dtype=BF16)
    slopes = jnp.asarray(_alibi_slopes())

    xf = x.reshape(TOKENS, D_MODEL)
    for l in range(DEPTH):
        proj = _in_projection(l, xf, g_pre_mix3, mod, w_in_p, bias_p3)
        bb, cre, cim, lam = _ssm_layout(lr[l], li[l], bbr[:, l], bbi[:, l], ssm_c_re[l], ssm_c_im[l])
        s = _ssm_mixer(l, proj.reshape(BATCH, SEQ, N_IN), perm, permt, bb, cre, cim, lam,
                       d_skip3, w_glu_b, b_glu3)
        a = _attention(proj, slopes)
        xf = _merge(l, s.reshape(TOKENS, SSM_WIDTH), a, proj, xf, mod, g_post_mix3, conv_mix_w,
                    w_ssm_out_b, w_attn_out_b, w_conv_out_b, w_o_b)
        act = _mlp_up(l, xf, g_pre_ffn3, mod, w_up_b, ffn_conv_w)
        xf = _mlp_down(l, act, w_down_b, xf, mod, g_post_ffn3)
    return xf.reshape(BATCH, SEQ, D_MODEL)
```

```python
import numpy as np
import jax
import jax.numpy as jnp
from jax import lax
from jax.experimental import pallas as pl
from jax.experimental.pallas import tpu as pltpu

F32 = jnp.float32
BF16 = jnp.bfloat16

D_MODEL = 2048
BATCH = 8
SEQ = 2048
DEPTH = 4
TOKENS = BATCH * SEQ
RMS_EPS = 1e-6
NEG_INF = -1e30
N_BRANCH = 3
SSM_WIDTH = 512
SSM_GROUP = 16
SSM_GROUPS = 32
SSM_STATE = 64
HEAD_DIM = 64
DSWA_PATTERNS = ((128, 1), (512, 4), (2048, 16))
ATTN_WIDTH = 512
HEADS_PER_PATTERN = 8
N_ATTN_HEADS = 24
QKV_WIDTH = 1536
CONV_WIDTH = 512
D_FF = 5632
OFF_U = 0
OFF_Q = SSM_WIDTH
OFF_K = OFF_Q + QKV_WIDTH
OFF_V = OFF_K + QKV_WIDTH
OFF_CONV = OFF_V + QKV_WIDTH
OFF_GATE = OFF_CONV + 3 * CONV_WIDTH
N_IN = OFF_GATE + N_BRANCH * D_MODEL

LANES = 128
VMEM_LIMIT = 60 * 1024 * 1024

TM_NORM = 1024
TN_PROJ = 1280
TF_UP = 512
UP_HALO = 16
TM_MERGE = 256
TN_MERGE = 512
TM_DOWN = 256
SSM_CHUNK = 128
QBLK = 128


def _cparams(sem):
    return pltpu.CompilerParams(dimension_semantics=sem, vmem_limit_bytes=VMEM_LIMIT)


def _resident(shape, index_map):
    return pl.BlockSpec(shape, index_map, pipeline_mode=pl.Buffered(1))


def _sigmoid(x):
    return 0.5 * jnp.tanh(0.5 * x) + 0.5


def _rows3(a):
    return a.reshape(DEPTH, 1, a.shape[-1])


def _norm_mod(x, g, scale_row, shift_row):
    ms = jnp.mean(x * x, axis=-1, keepdims=True)
    y = x * lax.rsqrt(ms + RMS_EPS) * g
    return (y * (1.0 + scale_row) + shift_row).astype(BF16)


def _mod_kernel(c_ref, w_ref, b_ref, o_ref):
    c = c_ref[...]
    cond = (c * jax.nn.sigmoid(c)).astype(BF16)
    o_ref[0] = jnp.dot(cond, w_ref[0].astype(BF16), preferred_element_type=F32) + b_ref[0]


def _modulation(c, w_mod, b_mod):
    tn = 1024
    n = 6 * D_MODEL
    return pl.pallas_call(
        _mod_kernel,
        out_shape=jax.ShapeDtypeStruct((DEPTH, BATCH, n), F32),
        grid=(DEPTH, n // tn),
        in_specs=[
            pl.BlockSpec((BATCH, D_MODEL), lambda l, j: (0, 0)),
            pl.BlockSpec((1, D_MODEL, tn), lambda l, j: (l, 0, j)),
            pl.BlockSpec((1, 1, tn), lambda l, j: (l, 0, j)),
        ],
        out_specs=pl.BlockSpec((1, BATCH, tn), lambda l, j: (l, 0, j)),
        compiler_params=_cparams(("arbitrary", "arbitrary")),
        name="modulation",
    )(c, w_mod, b_mod.reshape(DEPTH, 1, n))


def _norm_kernel(x_ref, g_ref, sh_ref, sc_ref, o_ref):
    b = pl.program_id(0) // (SEQ // TM_NORM)
    rows = 128

    def body(k, carry):
        r = pl.ds(pl.multiple_of(k * rows, rows), rows)
        o_ref[r, :] = _norm_mod(x_ref[r, :], g_ref[...], sc_ref[pl.ds(b, 1), :], sh_ref[pl.ds(b, 1), :])
        return carry

    lax.fori_loop(0, TM_NORM // rows, body, 0)


def _first_norm(x, g3, mod):
    d = D_MODEL
    return pl.pallas_call(
        _norm_kernel,
        out_shape=jax.ShapeDtypeStruct((TOKENS, d), BF16),
        grid=(TOKENS // TM_NORM,),
        in_specs=[
            pl.BlockSpec((TM_NORM, d), lambda i: (i, 0)),
            pl.BlockSpec((None, 1, d), lambda i: (0, 0, 0)),
            pl.BlockSpec((None, BATCH, d), lambda i: (0, 0, 0)),
            pl.BlockSpec((None, BATCH, d), lambda i: (0, 0, 1)),
        ],
        out_specs=pl.BlockSpec((TM_NORM, d), lambda i: (i, 0)),
        compiler_params=_cparams(("arbitrary",)),
        name="first_norm",
    )(x, g3, mod, mod)


def _inproj_kernel(h_ref, w_ref, bias_ref, o_ref):
    j = pl.program_id(1)
    raw_tiles = OFF_GATE // TN_PROJ
    raw_cols_in_mixed = OFF_GATE - raw_tiles * TN_PROJ

    @pl.when(j < raw_tiles)
    def _():
        o_ref[...] = jnp.dot(h_ref[...], w_ref[...], preferred_element_type=F32)

    @pl.when(j == raw_tiles)
    def _():
        acc = jnp.dot(h_ref[...], w_ref[...], preferred_element_type=F32)
        col = lax.broadcasted_iota(jnp.int32, acc.shape, 1)
        o_ref[...] = jnp.where(col < raw_cols_in_mixed, acc, _sigmoid(acc + bias_ref[...]))

    @pl.when(j > raw_tiles)
    def _():
        acc = jnp.dot(h_ref[...], w_ref[...], preferred_element_type=F32)
        o_ref[...] = _sigmoid(acc + bias_ref[...])


def _in_projection(l, h, w_in, bias3):
    d = D_MODEL
    return pl.pallas_call(
        _inproj_kernel,
        out_shape=jax.ShapeDtypeStruct((TOKENS, N_IN), F32),
        grid=(BATCH, N_IN // TN_PROJ),
        in_specs=[
            pl.BlockSpec((SEQ, d), lambda i, j: (i, 0)),
            pl.BlockSpec((None, d, TN_PROJ), lambda i, j: (l, 0, j)),
            pl.BlockSpec((None, 1, TN_PROJ), lambda i, j: (l, 0, j)),
        ],
        out_specs=pl.BlockSpec((SEQ, TN_PROJ), lambda i, j: (i, j)),
        compiler_params=_cparams(("arbitrary", "arbitrary")),
        name="in_projection",
    )(h, w_in, bias3)


def _ssm_param_kernel(ldt_ref, ar_ref, ai_ref, br_ref, bi_ref, lr_ref, li_ref, bbr_ref, bbi_ref):
    dt = jnp.exp(ldt_ref[...])
    ar = ar_ref[...]
    ai = ai_ref[...]
    mag = jnp.exp(ar * dt)
    lr = mag * jnp.cos(ai * dt)
    li = mag * jnp.sin(ai * dt)
    den = ar * ar + ai * ai
    fr = ((lr - 1.0) * ar + li * ai) / den
    fi = (li * ar - (lr - 1.0) * ai) / den
    lr_ref[...] = lr
    li_ref[...] = li
    for ch in range(SSM_GROUP):
        br = br_ref[ch]
        bi = bi_ref[ch]
        bbr_ref[ch] = fr * br - fi * bi
        bbi_ref[ch] = fr * bi + fi * br


def _ssm_params(log_dt, a_re, a_im, b_re, b_im):
    n = DEPTH * SSM_GROUPS
    mat = jax.ShapeDtypeStruct((n, SSM_STATE), F32)
    cube = jax.ShapeDtypeStruct((SSM_GROUP, n, SSM_STATE), F32)

    def channel_major(b):
        return b.transpose(3, 0, 1, 2).reshape(SSM_GROUP, n, SSM_STATE)

    lr, li, bbr, bbi = pl.pallas_call(
        _ssm_param_kernel,
        out_shape=(mat, mat, cube, cube),
        name="ssm_params",
    )(log_dt.reshape(n, 1), a_re.reshape(n, SSM_STATE), a_im.reshape(n, SSM_STATE),
      channel_major(b_re), channel_major(b_im))
    shape3 = (DEPTH, SSM_GROUPS, SSM_STATE)
    return (lr.reshape(shape3), li.reshape(shape3),
            bbr.reshape((SSM_GROUP,) + shape3), bbi.reshape((SSM_GROUP,) + shape3))


N_SLAB = SSM_WIDTH // LANES
GROUPS_PER_SLAB = LANES // SSM_GROUP
STATES_PER_SLAB = GROUPS_PER_SLAB * SSM_STATE
N_STATES = SSM_GROUPS * SSM_STATE


def _ssm_layout(lr, li, bbr, bbi, c_re, c_im):
    eye = jnp.eye(GROUPS_PER_SLAB, dtype=F32)

    def in_blocks(bb):
        t = bb.reshape(SSM_GROUP, N_SLAB, GROUPS_PER_SLAB, SSM_STATE).transpose(1, 2, 0, 3)
        blk = eye[None, :, None, :, None] * t[:, :, :, None, :]
        return blk.reshape(N_SLAB, LANES, STATES_PER_SLAB)

    def out_blocks(cc):
        t = cc.reshape(N_SLAB, GROUPS_PER_SLAB, SSM_GROUP, SSM_STATE).transpose(0, 1, 3, 2)
        blk = eye[None, :, None, :, None] * t[:, :, :, None, :]
        return blk.reshape(N_SLAB, STATES_PER_SLAB, LANES)

    bb = jnp.concatenate([in_blocks(bbr), in_blocks(bbi)], axis=-1).astype(BF16)
    lam = jnp.stack([lr.reshape(N_STATES), li.reshape(N_STATES)], axis=0)
    return bb, out_blocks(c_re).astype(BF16), out_blocks(c_im).astype(BF16), lam


def _ssm_kernel(u_ref, perm_ref, permt_ref, bb_ref, cre_ref, cim_ref, lam_ref, dsk_ref, wglu_ref, bglu_ref,
                o_ref, x_scr, h_scr):
    rows = BATCH * SSM_CHUNK

    @pl.when(pl.program_id(0) == 0)
    def _():
        h_scr[...] = jnp.zeros_like(h_scr)

    u = u_ref[...].reshape(rows, SSM_WIDTH).astype(BF16)
    u_tm = jnp.dot(perm_ref[...], u, preferred_element_type=F32)
    u_tm_b = u_tm.astype(BF16)

    for s in range(N_SLAB):
        xs = jnp.dot(u_tm_b[:, s * LANES:(s + 1) * LANES], bb_ref[s], preferred_element_type=F32)
        x_scr[:, s * STATES_PER_SLAB:(s + 1) * STATES_PER_SLAB] = xs[:, :STATES_PER_SLAB]
        x_scr[:, N_STATES + s * STATES_PER_SLAB:N_STATES + (s + 1) * STATES_PER_SLAB] = xs[:, STATES_PER_SLAB:]

    width = STATES_PER_SLAB
    for s in range(N_STATES // width):
        re_cols = pl.ds(s * width, width)
        im_cols = pl.ds(N_STATES + s * width, width)
        lr = jnp.broadcast_to(lam_ref[0:1, s * width:(s + 1) * width], (BATCH, width))
        li = jnp.broadcast_to(lam_ref[1:2, s * width:(s + 1) * width], (BATCH, width))

        def step(t, carry, re_cols=re_cols, im_cols=im_cols, lr=lr, li=li):
            hr, hi = carry
            r0 = pl.multiple_of(t * BATCH, BATCH)
            xr = x_scr[pl.ds(r0, BATCH), re_cols]
            xi = x_scr[pl.ds(r0, BATCH), im_cols]
            nr = lr * hr - li * hi + xr
            ni = lr * hi + li * hr + xi
            x_scr[pl.ds(r0, BATCH), re_cols] = nr
            x_scr[pl.ds(r0, BATCH), im_cols] = ni
            return nr, ni

        hr, hi = lax.fori_loop(0, SSM_CHUNK, step, (h_scr[:, re_cols], h_scr[:, im_cols]), unroll=4)
        h_scr[:, re_cols] = hr
        h_scr[:, im_cols] = hi

    ys = []
    for s in range(N_SLAB):
        h_re = x_scr[:, s * STATES_PER_SLAB:(s + 1) * STATES_PER_SLAB].astype(BF16)
        h_im = x_scr[:, N_STATES + s * STATES_PER_SLAB:N_STATES + (s + 1) * STATES_PER_SLAB].astype(BF16)
        ys.append(jnp.dot(h_re, cre_ref[s], preferred_element_type=F32)
                  - jnp.dot(h_im, cim_ref[s], preferred_element_type=F32))
    y = jnp.concatenate(ys, axis=-1) + dsk_ref[...] * u_tm
    g = jax.nn.gelu(y)
    gate = jnp.dot(g.astype(BF16), wglu_ref[...], preferred_element_type=F32) + bglu_ref[...]
    out_tm = (g * _sigmoid(gate)).astype(BF16)
    out = jnp.dot(permt_ref[...], out_tm, preferred_element_type=F32)
    o_ref[...] = out.astype(BF16).reshape(BATCH, SSM_CHUNK, SSM_WIDTH)


def _ssm_mixer(l, proj3, perm, permt, bb, cre, cim, lam, d_skip3, w_glu, b_glu3):
    rows = BATCH * SSM_CHUNK
    return pl.pallas_call(
        _ssm_kernel,
        out_shape=jax.ShapeDtypeStruct((BATCH, SEQ, SSM_WIDTH), BF16),
        grid=(SEQ // SSM_CHUNK,),
        in_specs=[
            pl.BlockSpec((BATCH, SSM_CHUNK, SSM_WIDTH), lambda c: (0, c, OFF_U // SSM_WIDTH)),
            _resident((rows, rows), lambda c: (0, 0)),
            _resident((rows, rows), lambda c: (0, 0)),
            _resident((N_SLAB, LANES, 2 * STATES_PER_SLAB), lambda c: (0, 0, 0)),
            _resident((N_SLAB, STATES_PER_SLAB, LANES), lambda c: (0, 0, 0)),
            _resident((N_SLAB, STATES_PER_SLAB, LANES), lambda c: (0, 0, 0)),
            _resident((2, N_STATES), lambda c: (0, 0)),
            _resident((None, 1, SSM_WIDTH), lambda c: (l, 0, 0)),
            _resident((None, SSM_WIDTH, SSM_WIDTH), lambda c: (l, 0, 0)),
            _resident((None, 1, SSM_WIDTH), lambda c: (l, 0, 0)),
        ],
        out_specs=pl.BlockSpec((BATCH, SSM_CHUNK, SSM_WIDTH), lambda c: (0, c, 0)),
        scratch_shapes=[pltpu.VMEM((rows, 2 * N_STATES), F32), pltpu.VMEM((BATCH, 2 * N_STATES), F32)],
        compiler_params=_cparams(("arbitrary",)),
        name="ssm_mixer",
    )(proj3, perm, permt, bb, cre, cim, lam, d_skip3, w_glu, b_glu3)


ATTN_UNROLL_LATER = {1: 5, 4: 6}
ATTN_UNROLL_FIRST = 8


def _attn_kernel(slopes_ref, q0, k0, v0, q1, k1, v1, q2, k2, v2, o_ref,
                 qp, kt, va, vb, o0, l0, o1, l1, o2, l2):
    pair = pl.program_id(1)
    n_qblk = SEQ // QBLK
    lane = lax.broadcasted_iota(jnp.int32, (QBLK, LANES), 1)
    head0 = lane < HEAD_DIM
    qi = lax.broadcasted_iota(jnp.int32, (QBLK, 2 * QBLK), 0)
    kj = lax.broadcasted_iota(jnp.int32, (QBLK, 2 * QBLK), 1)
    dist = QBLK + qi - kj
    valid = (dist >= 0) & (dist <= QBLK)
    dist = dist.astype(F32)

    groups = ((q0, k0, v0, o0, l0), (q1, k1, v1, o1, l1), (q2, k2, v2, o2, l2))
    for g, (_, dil) in enumerate(DSWA_PATTERNS):
        q_ref, k_ref, v_ref, o_scr, l_scr = groups[g]
        n_blocks = SEQ // (dil * QBLK)

        def rows(start, dil=dil):
            if dil == 1:
                return pl.ds(pl.multiple_of(start, QBLK), QBLK)
            return pl.ds(start, QBLK, stride=dil)

        def natural_rows(idx, rows=rows, n_blocks=n_blocks, dil=dil):
            return rows(dil * QBLK * (idx % n_blocks) + idx // n_blocks)

        def reorder(idx, carry, q_ref=q_ref, k_ref=k_ref, v_ref=v_ref, natural_rows=natural_rows):
            src = natural_rows(idx)
            dst = pl.ds(pl.multiple_of(idx * QBLK, QBLK), QBLK)
            qp[dst, :] = (q_ref[src, :] * (HEAD_DIM ** -0.5)).astype(BF16)
            kt[idx] = k_ref[src, :].T.astype(BF16)
            v = v_ref[src, :]
            va[dst, :] = jnp.where(head0, v, 1.0).astype(BF16)
            vb[dst, :] = jnp.where(head0, 1.0, v).astype(BF16)
            return carry

        lax.fori_loop(0, n_qblk, reorder, 0, unroll=4)

        bias = []
        for hh in range(2):
            slope = slopes_ref[g * HEADS_PER_PATTERN + 2 * pair + hh] * float(dil)
            bias.append(jnp.where(valid, -slope * dist, NEG_INF))

        def block(idx, with_prev, o_scr=o_scr, l_scr=l_scr, bias=bias, natural_rows=natural_rows):
            cur = pl.ds(pl.multiple_of(idx * QBLK, QBLK), QBLK)
            q = qp[cur, :]
            if with_prev:
                keys_t = jnp.concatenate([kt[idx - 1], kt[idx]], axis=1)
                kv_rows = pl.ds(pl.multiple_of((idx - 1) * QBLK, QBLK), 2 * QBLK)
            else:
                keys_t = kt[idx]
                kv_rows = cur
            pvs, ms = [], []
            for hh, v_scr in ((0, va), (1, vb)):
                qh = jnp.where(head0 == (hh == 0), q, jnp.zeros_like(q))
                s = jnp.dot(qh, keys_t, preferred_element_type=F32)
                s = s + (bias[hh] if with_prev else bias[hh][:, QBLK:])
                m = jnp.max(s, axis=-1, keepdims=True)
                p = jnp.exp(s - m).astype(BF16)
                pvs.append(jnp.dot(p, v_scr[kv_rows, :], preferred_element_type=F32))
                ms.append(m)
            num = jnp.where(head0, pvs[0], pvs[1])
            den = pltpu.roll(jnp.where(head0, pvs[1], pvs[0]), HEAD_DIM, axis=1)
            dst = natural_rows(idx)
            o_scr[dst, :] = num / den
            l_scr[dst, :] = jnp.where(head0, ms[0], ms[1]) + jnp.log(den)

        def first_blocks(res, carry, block=block, n_blocks=n_blocks):
            block(res * n_blocks, False)
            return carry

        lax.fori_loop(0, dil, first_blocks, 0, unroll=min(dil, ATTN_UNROLL_FIRST))

        if n_blocks > 1:
            later = n_blocks - 1

            def later_blocks(k, carry, block=block, later=later, n_blocks=n_blocks):
                block((k // later) * n_blocks + k % later + 1, True)
                return carry

            lax.fori_loop(0, dil * later, later_blocks, 0, unroll=ATTN_UNROLL_LATER[dil])

    def merge(k, carry):
        r = pl.ds(pl.multiple_of(k * QBLK, QBLK), QBLK)
        la, lb, lc = l0[r, :], l1[r, :], l2[r, :]
        m = jnp.maximum(jnp.maximum(la, lb), lc)
        wa, wb, wc = jnp.exp(la - m), jnp.exp(lb - m), jnp.exp(lc - m)
        tot = wa + wb + wc
        o_ref[r, :] = ((wa * o0[r, :] + wb * o1[r, :] + wc * o2[r, :]) / tot).astype(BF16)
        return carry

    lax.fori_loop(0, n_qblk, merge, 0, unroll=2)


def _attention(proj, slopes):
    pairs = HEADS_PER_PATTERN // 2

    def col(base, g):
        return lambda b, p: (b, (base + g * HEADS_PER_PATTERN * HEAD_DIM) // LANES + p)

    in_specs = [pl.BlockSpec(memory_space=pltpu.SMEM)]
    for g in range(len(DSWA_PATTERNS)):
        for base in (OFF_Q, OFF_K, OFF_V):
            in_specs.append(pl.BlockSpec((SEQ, LANES), col(base, g)))
    scr = pltpu.VMEM((SEQ, LANES), F32)
    seq_b = pltpu.VMEM((SEQ, LANES), BF16)
    kt_b = pltpu.VMEM((SEQ // QBLK, LANES, QBLK), BF16)
    return pl.pallas_call(
        _attn_kernel,
        out_shape=jax.ShapeDtypeStruct((TOKENS, ATTN_WIDTH), BF16),
        grid=(BATCH, pairs),
        in_specs=in_specs,
        out_specs=pl.BlockSpec((SEQ, LANES), lambda b, p: (b, p)),
        scratch_shapes=[seq_b, kt_b, seq_b, seq_b] + [scr] * 6,
        compiler_params=_cparams(("arbitrary", "arbitrary")),
        name="dilated_attention",
    )(slopes, *([proj] * 9))


N_MERGE_CHUNKS = D_MODEL // TN_MERGE


def _merge_kernel(*refs):
    (s_ref, a_ref, cb_ref, cc_ref, ch_ref, cch_ref, chh_ref), refs = refs[:7], refs[7:]
    gate_refs, refs = refs[:N_BRANCH * N_MERGE_CHUNKS], refs[N_BRANCH * N_MERGE_CHUNKS:]
    (x_ref, gt_ref, sh2_ref, sc2_ref, gpost_ref, gffn_ref, cw_ref, wss_ref, wat_ref, wcv_ref, wo_ref,
     o_ref, h2_ref, z_scr, m_scr) = refs
    i = pl.program_id(0)
    tiles_per_seq = SEQ // TM_MERGE
    b = i // tiles_per_seq
    seq_start = (i % tiles_per_seq) == 0

    halo = cch_ref[...] * chh_ref[...]
    z_scr[0:8, :] = jnp.where(seq_start, 0.0, halo)
    z_scr[8:8 + TM_MERGE, :] = cc_ref[...] * ch_ref[...]
    conv = (cw_ref[0:1, :] * z_scr[pl.ds(8, TM_MERGE), :]
            + cw_ref[1:2, :] * z_scr[pl.ds(7, TM_MERGE), :]
            + cw_ref[2:3, :] * z_scr[pl.ds(6, TM_MERGE), :])
    cv = (cb_ref[...] * conv).astype(BF16)

    s = s_ref[...]
    a = a_ref[...]
    for c in range(N_MERGE_CHUNKS):
        cols = slice(c * TN_MERGE, (c + 1) * TN_MERGE)
        y_ssm = jnp.dot(s, wss_ref[:, cols], preferred_element_type=F32)
        y_att = jnp.dot(a, wat_ref[:, cols], preferred_element_type=F32)
        y_cv = jnp.dot(cv, wcv_ref[:, cols], preferred_element_type=F32)
        merged = (gate_refs[c][...] * y_ssm + gate_refs[N_MERGE_CHUNKS + c][...] * y_att
                  + gate_refs[2 * N_MERGE_CHUNKS + c][...] * y_cv)
        m_scr[:, cols] = merged.astype(BF16)

    y = jnp.dot(m_scr[...], wo_ref[...], preferred_element_type=F32)
    ms = jnp.mean(y * y, axis=-1, keepdims=True)
    yn = y * lax.rsqrt(ms + RMS_EPS) * gpost_ref[...]
    x_new = x_ref[...] + gt_ref[pl.ds(b, 1), :] * yn
    o_ref[...] = x_new
    h2_ref[...] = _norm_mod(x_new, gffn_ref[...], sc2_ref[pl.ds(b, 1), :], sh2_ref[pl.ds(b, 1), :])


def _merge(l, s, a, proj, x, mod, g_post3, g_ffn3, conv_w, w_ssm_out, w_attn_out, w_conv_out, w_o):
    tm = TM_MERGE
    d = D_MODEL
    cw = CONV_WIDTH
    cb_blk = OFF_CONV // cw
    halo_blocks = tm // 8

    def halo_map(col_blk):
        return lambda i: (jnp.maximum(i * halo_blocks - 1, 0), col_blk)

    def gate_spec(branch, chunk):
        blk = (OFF_GATE + branch * d) // TN_MERGE + chunk
        return pl.BlockSpec((tm, TN_MERGE), lambda i: (i, blk))

    gate_specs = [gate_spec(br, c) for br in range(N_BRANCH) for c in range(N_MERGE_CHUNKS)]
    tile = pl.BlockSpec((tm, d), lambda i: (i, 0))
    return pl.pallas_call(
        _merge_kernel,
        out_shape=(jax.ShapeDtypeStruct((TOKENS, d), F32), jax.ShapeDtypeStruct((TOKENS, d), BF16)),
        grid=(TOKENS // tm,),
        in_specs=[
            pl.BlockSpec((tm, SSM_WIDTH), lambda i: (i, 0)),
            pl.BlockSpec((tm, ATTN_WIDTH), lambda i: (i, 0)),
            pl.BlockSpec((tm, cw), lambda i: (i, cb_blk)),
            pl.BlockSpec((tm, cw), lambda i: (i, cb_blk + 1)),
            pl.BlockSpec((tm, cw), lambda i: (i, cb_blk + 2)),
            pl.BlockSpec((8, cw), halo_map(cb_blk + 1)),
            pl.BlockSpec((8, cw), halo_map(cb_blk + 2)),
        ] + gate_specs + [
            tile,
            pl.BlockSpec((None, BATCH, d), lambda i: (l, 0, 2)),
            pl.BlockSpec((None, BATCH, d), lambda i: (l, 0, 3)),
            pl.BlockSpec((None, BATCH, d), lambda i: (l, 0, 4)),
            _resident((None, 1, d), lambda i: (l, 0, 0)),
            _resident((None, 1, d), lambda i: (l, 0, 0)),
            _resident((None, 3, cw), lambda i: (l, 0, 0)),
            _resident((None, SSM_WIDTH, d), lambda i: (l, 0, 0)),
            _resident((None, ATTN_WIDTH, d), lambda i: (l, 0, 0)),
            _resident((None, cw, d), lambda i: (l, 0, 0)),
            _resident((None, d, d), lambda i: (l, 0, 0)),
        ],
        out_specs=(tile, tile),
        scratch_shapes=[pltpu.VMEM((tm + 8, cw), F32), pltpu.VMEM((tm, d), BF16)],
        compiler_params=_cparams(("arbitrary",)),
        name="branch_merge",
    )(s, a, *([proj] * (5 + len(gate_specs))), x, mod, mod, mod, g_post3, g_ffn3, conv_w,
      w_ssm_out, w_attn_out, w_conv_out, w_o)


def _up_kernel(h_ref, wa_ref, wb_ref, cwa_ref, cwb_ref, o_ref):
    half = SEQ // 2
    row = lax.broadcasted_iota(jnp.int32, (half, 1), 0)
    for first in (True, False):
        lead = 0 if first else UP_HALO
        h = h_ref[pl.ds(0 if first else half - UP_HALO, half + lead), :]
        convs = []
        for w_ref, cw_ref in ((wa_ref, cwa_ref), (wb_ref, cwb_ref)):
            z = jnp.dot(h, w_ref[...], preferred_element_type=F32)
            z1 = pltpu.roll(z, 1, axis=0)
            z2 = pltpu.roll(z, 2, axis=0)
            if first:
                z1 = jnp.where(row >= 1, z1, 0.0)
                z2 = jnp.where(row >= 2, z2, 0.0)
            y = cw_ref[0:1, :] * z + cw_ref[1:2, :] * z1 + cw_ref[2:3, :] * z2
            convs.append(y[lead:, :])
        ca, cb = convs
        o_ref[pl.ds(0 if first else half, half), :] = (ca * _sigmoid(ca) * cb).astype(BF16)


def _mlp_up(l, h, w_up, conv_w):
    d = D_MODEL
    tf = TF_UP
    nf = D_FF // tf
    return pl.pallas_call(
        _up_kernel,
        out_shape=jax.ShapeDtypeStruct((TOKENS, D_FF), BF16),
        grid=(BATCH, nf),
        in_specs=[
            pl.BlockSpec((SEQ, d), lambda i, j: (i, 0)),
            pl.BlockSpec((None, d, tf), lambda i, j: (l, 0, j)),
            pl.BlockSpec((None, d, tf), lambda i, j: (l, 0, nf + j)),
            pl.BlockSpec((None, 3, tf), lambda i, j: (l, 0, j)),
            pl.BlockSpec((None, 3, tf), lambda i, j: (l, 0, nf + j)),
        ],
        out_specs=pl.BlockSpec((SEQ, tf), lambda i, j: (i, j)),
        compiler_params=_cparams(("arbitrary", "arbitrary")),
        name="mlp_up",
    )(h, w_up, w_up, conv_w, conv_w)


def _down_kernel(act_ref, w_ref, x_ref, gt_ref, gpost_ref, *rest):
    b = pl.program_id(0) // (SEQ // TM_DOWN)
    y = jnp.dot(act_ref[...], w_ref[...], preferred_element_type=F32)
    ms = jnp.mean(y * y, axis=-1, keepdims=True)
    yn = y * lax.rsqrt(ms + RMS_EPS) * gpost_ref[...]
    x_new = x_ref[...] + gt_ref[pl.ds(b, 1), :] * yn
    if len(rest) == 1:
        (o_ref,) = rest
        o_ref[...] = x_new
    else:
        sh_ref, sc_ref, gnext_ref, o_ref, hn_ref = rest
        o_ref[...] = x_new
        hn_ref[...] = _norm_mod(x_new, gnext_ref[...], sc_ref[pl.ds(b, 1), :], sh_ref[pl.ds(b, 1), :])


def _mlp_down(l, act, w_down, x, mod, g_post3, g_mix3):
    d = D_MODEL
    tm = TM_DOWN
    last = l == DEPTH - 1
    tile = pl.BlockSpec((tm, d), lambda i: (i, 0))
    in_specs = [
        pl.BlockSpec((tm, D_FF), lambda i: (i, 0)),
        _resident((None, D_FF, d), lambda i: (l, 0, 0)),
        tile,
        pl.BlockSpec((None, BATCH, d), lambda i: (l, 0, 5)),
        _resident((None, 1, d), lambda i: (l, 0, 0)),
    ]
    args = [act, w_down, x, mod, g_post3]
    x_shape = jax.ShapeDtypeStruct((TOKENS, d), F32)
    if last:
        out_shape, out_specs = x_shape, tile
    else:
        in_specs += [
            pl.BlockSpec((None, BATCH, d), lambda i: (l + 1, 0, 0)),
            pl.BlockSpec((None, BATCH, d), lambda i: (l + 1, 0, 1)),
            _resident((None, 1, d), lambda i: (l + 1, 0, 0)),
        ]
        args += [mod, mod, g_mix3]
        out_shape, out_specs = (x_shape, jax.ShapeDtypeStruct((TOKENS, d), BF16)), (tile, tile)
    out = pl.pallas_call(
        _down_kernel,
        out_shape=out_shape,
        grid=(TOKENS // tm,),
        in_specs=in_specs,
        out_specs=out_specs,
        compiler_params=_cparams(("arbitrary",)),
        name="mlp_down",
    )(*args)
    return (out, None) if last else out


def _alibi_slopes():
    h = N_ATTN_HEADS
    return np.array([2.0 ** (-8.0 * (i + 1) / h) for i in range(h)], dtype=np.float32)


def _time_major_permutation():
    rows = BATCH * SSM_CHUNK
    i = np.arange(rows)
    src = (i % BATCH) * SSM_CHUNK + i // BATCH
    p = np.zeros((rows, rows), np.float32)
    p[i, src] = 1.0
    return p


def kernel(x, c, w_mod, b_mod, g_pre_mix, g_post_mix, g_pre_ffn, g_post_ffn, w_in, ssm_log_dt, ssm_a_re,
           ssm_a_im, ssm_b_re, ssm_b_im, ssm_c_re, ssm_c_im, ssm_d, w_glu, b_glu, conv_mix_w, w_ssm_out,
           w_attn_out, w_conv_out, b_gate, w_o, w_up, ffn_conv_w, w_down):
    w_in_b = w_in.astype(BF16)
    bias3 = _rows3(jnp.concatenate([jnp.zeros((DEPTH, OFF_GATE), F32), b_gate], axis=-1))
    w_glu_b = w_glu.astype(BF16)
    w_ssm_out_b = w_ssm_out.astype(BF16)
    w_attn_out_b = w_attn_out.astype(BF16)
    w_conv_out_b = w_conv_out.astype(BF16)
    w_o_b = w_o.astype(BF16)
    w_up_b = w_up.astype(BF16)
    w_down_b = w_down.astype(BF16)
    g_pre_mix3, g_post_mix3 = _rows3(g_pre_mix), _rows3(g_post_mix)
    g_pre_ffn3, g_post_ffn3 = _rows3(g_pre_ffn), _rows3(g_post_ffn)
    d_skip3, b_glu3 = _rows3(ssm_d), _rows3(b_glu)

    mod = _modulation(c, w_mod, b_mod)
    lr, li, bbr, bbi = _ssm_params(ssm_log_dt, ssm_a_re, ssm_a_im, ssm_b_re, ssm_b_im)
    perm_np = _time_major_permutation()
    perm = jnp.asarray(perm_np, dtype=BF16)
    permt = jnp.asarray(perm_np.T, dtype=BF16)
    slopes = jnp.asarray(_alibi_slopes())

    xf = x.reshape(TOKENS, D_MODEL)
    h = _first_norm(xf, g_pre_mix3, mod)
    for l in range(DEPTH):
        proj = _in_projection(l, h, w_in_b, bias3)
        bb, cre, cim, lam = _ssm_layout(lr[l], li[l], bbr[:, l], bbi[:, l], ssm_c_re[l], ssm_c_im[l])
        s = _ssm_mixer(l, proj.reshape(BATCH, SEQ, N_IN), perm, permt, bb, cre, cim, lam,
                       d_skip3, w_glu_b, b_glu3)
        a = _attention(proj, slopes)
        xf, h = _merge(l, s.reshape(TOKENS, SSM_WIDTH), a, proj, xf, mod, g_post_mix3, g_pre_ffn3, conv_mix_w,
                       w_ssm_out_b, w_attn_out_b, w_conv_out_b, w_o_b)
        act = _mlp_up(l, h, w_up_b, ffn_conv_w)
        xf, h = _mlp_down(l, act, w_down_b, xf, mod, g_post_ffn3, g_pre_mix3)
    return xf.reshape(BATCH, SEQ, D_MODEL)
```

```python
import numpy as np
import jax
import jax.numpy as jnp
from jax import lax
from jax.experimental import pallas as pl
from jax.experimental.pallas import tpu as pltpu

F32 = jnp.float32
BF16 = jnp.bfloat16

D_MODEL = 2048
BATCH = 8
SEQ = 2048
DEPTH = 4
TOKENS = BATCH * SEQ
RMS_EPS = 1e-6
NEG_INF = -1e30
N_BRANCH = 3
SSM_WIDTH = 512
SSM_GROUP = 16
SSM_GROUPS = 32
SSM_STATE = 64
HEAD_DIM = 64
DSWA_PATTERNS = ((128, 1), (512, 4), (2048, 16))
ATTN_WIDTH = 512
HEADS_PER_PATTERN = 8
N_ATTN_HEADS = 24
QKV_WIDTH = 1536
CONV_WIDTH = 512
D_FF = 5632
OFF_U = 0
OFF_Q = SSM_WIDTH
OFF_K = OFF_Q + QKV_WIDTH
OFF_V = OFF_K + QKV_WIDTH
OFF_CONV = OFF_V + QKV_WIDTH
OFF_GATE = OFF_CONV + 3 * CONV_WIDTH
N_IN = OFF_GATE + N_BRANCH * D_MODEL

LANES = 128
VMEM_LIMIT = 60 * 1024 * 1024

TM_NORM = 1024
TN_PROJ = 1280
TF_UP = 512
UP_HALO = 16
TM_MERGE = 256
TN_MERGE = 512
TM_DOWN = 256
SSM_CHUNK = 128
QBLK = 128


def _cparams(sem):
    return pltpu.CompilerParams(dimension_semantics=sem, vmem_limit_bytes=VMEM_LIMIT)


def _resident(shape, index_map):
    return pl.BlockSpec(shape, index_map, pipeline_mode=pl.Buffered(1))


def _sigmoid(x):
    return 0.5 * jnp.tanh(0.5 * x) + 0.5


def _rows3(a):
    return a.reshape(DEPTH, 1, a.shape[-1])


def _norm_mod(x, g, scale_row, shift_row):
    ms = jnp.mean(x * x, axis=-1, keepdims=True)
    y = x * lax.rsqrt(ms + RMS_EPS) * g
    return (y * (1.0 + scale_row) + shift_row).astype(BF16)


def _mod_kernel(c_ref, w_ref, b_ref, o_ref):
    c = c_ref[...]
    cond = (c * jax.nn.sigmoid(c)).astype(BF16)
    o_ref[0] = jnp.dot(cond, w_ref[0].astype(BF16), preferred_element_type=F32) + b_ref[0]


def _modulation(c, w_mod, b_mod):
    tn = 1024
    n = 6 * D_MODEL
    return pl.pallas_call(
        _mod_kernel,
        out_shape=jax.ShapeDtypeStruct((DEPTH, BATCH, n), F32),
        grid=(DEPTH, n // tn),
        in_specs=[
            pl.BlockSpec((BATCH, D_MODEL), lambda l, j: (0, 0)),
            pl.BlockSpec((1, D_MODEL, tn), lambda l, j: (l, 0, j)),
            pl.BlockSpec((1, 1, tn), lambda l, j: (l, 0, j)),
        ],
        out_specs=pl.BlockSpec((1, BATCH, tn), lambda l, j: (l, 0, j)),
        compiler_params=_cparams(("arbitrary", "arbitrary")),
        name="modulation",
    )(c, w_mod, b_mod.reshape(DEPTH, 1, n))


def _norm_kernel(x_ref, g_ref, sh_ref, sc_ref, o_ref):
    b = pl.program_id(0) // (SEQ // TM_NORM)
    rows = 128

    def body(k, carry):
        r = pl.ds(pl.multiple_of(k * rows, rows), rows)
        o_ref[r, :] = _norm_mod(x_ref[r, :], g_ref[...], sc_ref[pl.ds(b, 1), :], sh_ref[pl.ds(b, 1), :])
        return carry

    lax.fori_loop(0, TM_NORM // rows, body, 0)


def _first_norm(x, g3, mod):
    d = D_MODEL
    return pl.pallas_call(
        _norm_kernel,
        out_shape=jax.ShapeDtypeStruct((TOKENS, d), BF16),
        grid=(TOKENS // TM_NORM,),
        in_specs=[
            pl.BlockSpec((TM_NORM, d), lambda i: (i, 0)),
            pl.BlockSpec((None, 1, d), lambda i: (0, 0, 0)),
            pl.BlockSpec((None, BATCH, d), lambda i: (0, 0, 0)),
            pl.BlockSpec((None, BATCH, d), lambda i: (0, 0, 1)),
        ],
        out_specs=pl.BlockSpec((TM_NORM, d), lambda i: (i, 0)),
        compiler_params=_cparams(("arbitrary",)),
        name="first_norm",
    )(x, g3, mod, mod)


def _inproj_kernel(h_ref, w_ref, bias_ref, o_ref):
    j = pl.program_id(1)
    raw_tiles = OFF_GATE // TN_PROJ
    raw_cols_in_mixed = OFF_GATE - raw_tiles * TN_PROJ

    @pl.when(j < raw_tiles)
    def _():
        o_ref[...] = jnp.dot(h_ref[...], w_ref[...], preferred_element_type=F32)

    @pl.when(j == raw_tiles)
    def _():
        acc = jnp.dot(h_ref[...], w_ref[...], preferred_element_type=F32)
        col = lax.broadcasted_iota(jnp.int32, acc.shape, 1)
        o_ref[...] = jnp.where(col < raw_cols_in_mixed, acc, _sigmoid(acc + bias_ref[...]))

    @pl.when(j > raw_tiles)
    def _():
        acc = jnp.dot(h_ref[...], w_ref[...], preferred_element_type=F32)
        o_ref[...] = _sigmoid(acc + bias_ref[...])


def _in_projection(l, h, w_in, bias3):
    d = D_MODEL
    return pl.pallas_call(
        _inproj_kernel,
        out_shape=jax.ShapeDtypeStruct((TOKENS, N_IN), F32),
        grid=(BATCH, N_IN // TN_PROJ),
        in_specs=[
            pl.BlockSpec((SEQ, d), lambda i, j: (i, 0)),
            pl.BlockSpec((None, d, TN_PROJ), lambda i, j: (l, 0, j)),
            pl.BlockSpec((None, 1, TN_PROJ), lambda i, j: (l, 0, j)),
        ],
        out_specs=pl.BlockSpec((SEQ, TN_PROJ), lambda i, j: (i, j)),
        compiler_params=_cparams(("arbitrary", "arbitrary")),
        name="in_projection",
    )(h, w_in, bias3)


def _ssm_param_kernel(ldt_ref, ar_ref, ai_ref, br_ref, bi_ref, lr_ref, li_ref, bbr_ref, bbi_ref):
    dt = jnp.exp(ldt_ref[...])
    ar = ar_ref[...]
    ai = ai_ref[...]
    mag = jnp.exp(ar * dt)
    lr = mag * jnp.cos(ai * dt)
    li = mag * jnp.sin(ai * dt)
    den = ar * ar + ai * ai
    fr = ((lr - 1.0) * ar + li * ai) / den
    fi = (li * ar - (lr - 1.0) * ai) / den
    lr_ref[...] = lr
    li_ref[...] = li
    for ch in range(SSM_GROUP):
        br = br_ref[ch]
        bi = bi_ref[ch]
        bbr_ref[ch] = fr * br - fi * bi
        bbi_ref[ch] = fr * bi + fi * br


def _ssm_params(log_dt, a_re, a_im, b_re, b_im):
    n = DEPTH * SSM_GROUPS
    mat = jax.ShapeDtypeStruct((n, SSM_STATE), F32)
    cube = jax.ShapeDtypeStruct((SSM_GROUP, n, SSM_STATE), F32)

    def channel_major(b):
        return b.transpose(3, 0, 1, 2).reshape(SSM_GROUP, n, SSM_STATE)

    lr, li, bbr, bbi = pl.pallas_call(
        _ssm_param_kernel,
        out_shape=(mat, mat, cube, cube),
        name="ssm_params",
    )(log_dt.reshape(n, 1), a_re.reshape(n, SSM_STATE), a_im.reshape(n, SSM_STATE),
      channel_major(b_re), channel_major(b_im))
    shape3 = (DEPTH, SSM_GROUPS, SSM_STATE)
    return (lr.reshape(shape3), li.reshape(shape3),
            bbr.reshape((SSM_GROUP,) + shape3), bbi.reshape((SSM_GROUP,) + shape3))


N_SLAB = SSM_WIDTH // LANES
GROUPS_PER_SLAB = LANES // SSM_GROUP
STATES_PER_SLAB = GROUPS_PER_SLAB * SSM_STATE
N_STATES = SSM_GROUPS * SSM_STATE


def _ssm_layout(lr, li, bbr, bbi, c_re, c_im):
    eye = jnp.eye(GROUPS_PER_SLAB, dtype=F32)

    def in_blocks(bb):
        t = bb.reshape(SSM_GROUP, N_SLAB, GROUPS_PER_SLAB, SSM_STATE).transpose(1, 2, 0, 3)
        blk = eye[None, :, None, :, None] * t[:, :, :, None, :]
        return blk.reshape(N_SLAB, LANES, STATES_PER_SLAB)

    def out_blocks(cc):
        t = cc.reshape(N_SLAB, GROUPS_PER_SLAB, SSM_GROUP, SSM_STATE).transpose(0, 1, 3, 2)
        blk = eye[None, :, None, :, None] * t[:, :, :, None, :]
        return blk.reshape(N_SLAB, STATES_PER_SLAB, LANES)

    bb = jnp.concatenate([in_blocks(bbr), in_blocks(bbi)], axis=-1).astype(BF16)
    lam = jnp.stack([lr.reshape(N_STATES), li.reshape(N_STATES)], axis=0)
    return bb, out_blocks(c_re).astype(BF16), out_blocks(c_im).astype(BF16), lam


def _ssm_kernel(u_ref, perm_ref, bb_ref, cre_ref, cim_ref, lam_ref, dsk_ref, wglu_ref, bglu_ref,
                o_ref, x_scr, h_scr):
    rows = BATCH * SSM_CHUNK

    @pl.when(pl.program_id(0) == 0)
    def _():
        h_scr[...] = jnp.zeros_like(h_scr)

    u = u_ref[...].reshape(rows, SSM_WIDTH).astype(BF16)
    u_tm = jnp.dot(perm_ref[...], u, preferred_element_type=F32)
    u_tm_b = u_tm.astype(BF16)

    for s in range(N_SLAB):
        xs = jnp.dot(u_tm_b[:, s * LANES:(s + 1) * LANES], bb_ref[s], preferred_element_type=F32)
        x_scr[:, s * STATES_PER_SLAB:(s + 1) * STATES_PER_SLAB] = xs[:, :STATES_PER_SLAB]
        x_scr[:, N_STATES + s * STATES_PER_SLAB:N_STATES + (s + 1) * STATES_PER_SLAB] = xs[:, STATES_PER_SLAB:]

    width = STATES_PER_SLAB
    for s in range(N_STATES // width):
        re_cols = pl.ds(s * width, width)
        im_cols = pl.ds(N_STATES + s * width, width)
        lr = jnp.broadcast_to(lam_ref[0:1, s * width:(s + 1) * width], (BATCH, width))
        li = jnp.broadcast_to(lam_ref[1:2, s * width:(s + 1) * width], (BATCH, width))

        def step(t, carry, re_cols=re_cols, im_cols=im_cols, lr=lr, li=li):
            hr, hi = carry
            r0 = pl.multiple_of(t * BATCH, BATCH)
            xr = x_scr[pl.ds(r0, BATCH), re_cols]
            xi = x_scr[pl.ds(r0, BATCH), im_cols]
            nr = lr * hr - li * hi + xr
            ni = lr * hi + li * hr + xi
            x_scr[pl.ds(r0, BATCH), re_cols] = nr
            x_scr[pl.ds(r0, BATCH), im_cols] = ni
            return nr, ni

        hr, hi = lax.fori_loop(0, SSM_CHUNK, step, (h_scr[:, re_cols], h_scr[:, im_cols]), unroll=4)
        h_scr[:, re_cols] = hr
        h_scr[:, im_cols] = hi

    ys = []
    for s in range(N_SLAB):
        h_re = x_scr[:, s * STATES_PER_SLAB:(s + 1) * STATES_PER_SLAB].astype(BF16)
        h_im = x_scr[:, N_STATES + s * STATES_PER_SLAB:N_STATES + (s + 1) * STATES_PER_SLAB].astype(BF16)
        ys.append(jnp.dot(h_re, cre_ref[s], preferred_element_type=F32)
                  - jnp.dot(h_im, cim_ref[s], preferred_element_type=F32))
    y = jnp.concatenate(ys, axis=-1) + dsk_ref[...] * u_tm
    g = jax.nn.gelu(y)
    gate = jnp.dot(g.astype(BF16), wglu_ref[...], preferred_element_type=F32) + bglu_ref[...]
    o_ref[...] = (g * _sigmoid(gate)).astype(BF16)


def _ssm_mixer(l, proj3, perm, bb, cre, cim, lam, d_skip3, w_glu, b_glu3):
    rows = BATCH * SSM_CHUNK
    return pl.pallas_call(
        _ssm_kernel,
        out_shape=jax.ShapeDtypeStruct((SEQ * BATCH, SSM_WIDTH), BF16),
        grid=(SEQ // SSM_CHUNK,),
        in_specs=[
            pl.BlockSpec((BATCH, SSM_CHUNK, SSM_WIDTH), lambda c: (0, c, OFF_U // SSM_WIDTH)),
            _resident((rows, rows), lambda c: (0, 0)),
            _resident((N_SLAB, LANES, 2 * STATES_PER_SLAB), lambda c: (0, 0, 0)),
            _resident((N_SLAB, STATES_PER_SLAB, LANES), lambda c: (0, 0, 0)),
            _resident((N_SLAB, STATES_PER_SLAB, LANES), lambda c: (0, 0, 0)),
            _resident((2, N_STATES), lambda c: (0, 0)),
            _resident((None, 1, SSM_WIDTH), lambda c: (l, 0, 0)),
            _resident((None, SSM_WIDTH, SSM_WIDTH), lambda c: (l, 0, 0)),
            _resident((None, 1, SSM_WIDTH), lambda c: (l, 0, 0)),
        ],
        out_specs=pl.BlockSpec((rows, SSM_WIDTH), lambda c: (c, 0)),
        scratch_shapes=[pltpu.VMEM((rows, 2 * N_STATES), F32), pltpu.VMEM((BATCH, 2 * N_STATES), F32)],
        compiler_params=_cparams(("arbitrary",)),
        name="ssm_mixer",
    )(proj3, perm, bb, cre, cim, lam, d_skip3, w_glu, b_glu3)


ATTN_UNROLL_LATER = {1: 5, 4: 6}
ATTN_UNROLL_FIRST = 8


def _attn_kernel(slopes_ref, q0, k0, v0, q1, k1, v1, q2, k2, v2, o_ref,
                 qp, kt, va, vb, o0, l0, o1, l1, o2, l2):
    pair = pl.program_id(1)
    n_qblk = SEQ // QBLK
    lane = lax.broadcasted_iota(jnp.int32, (QBLK, LANES), 1)
    head0 = lane < HEAD_DIM
    qi = lax.broadcasted_iota(jnp.int32, (QBLK, 2 * QBLK), 0)
    kj = lax.broadcasted_iota(jnp.int32, (QBLK, 2 * QBLK), 1)
    dist = QBLK + qi - kj
    valid = (dist >= 0) & (dist <= QBLK)
    dist = dist.astype(F32)

    groups = ((q0, k0, v0, o0, l0), (q1, k1, v1, o1, l1), (q2, k2, v2, o2, l2))
    for g, (_, dil) in enumerate(DSWA_PATTERNS):
        q_ref, k_ref, v_ref, o_scr, l_scr = groups[g]
        n_blocks = SEQ // (dil * QBLK)

        def rows(start, dil=dil):
            if dil == 1:
                return pl.ds(pl.multiple_of(start, QBLK), QBLK)
            return pl.ds(start, QBLK, stride=dil)

        def natural_rows(idx, rows=rows, n_blocks=n_blocks, dil=dil):
            return rows(dil * QBLK * (idx % n_blocks) + idx // n_blocks)

        def reorder(idx, carry, q_ref=q_ref, k_ref=k_ref, v_ref=v_ref, natural_rows=natural_rows):
            src = natural_rows(idx)
            dst = pl.ds(pl.multiple_of(idx * QBLK, QBLK), QBLK)
            qp[dst, :] = (q_ref[src, :] * (HEAD_DIM ** -0.5)).astype(BF16)
            kt[idx] = k_ref[src, :].T.astype(BF16)
            v = v_ref[src, :]
            va[dst, :] = jnp.where(head0, v, 1.0).astype(BF16)
            vb[dst, :] = jnp.where(head0, 1.0, v).astype(BF16)
            return carry

        lax.fori_loop(0, n_qblk, reorder, 0, unroll=4)

        bias = []
        for hh in range(2):
            slope = slopes_ref[g * HEADS_PER_PATTERN + 2 * pair + hh] * float(dil)
            bias.append(jnp.where(valid, -slope * dist, NEG_INF))

        def block(idx, with_prev, o_scr=o_scr, l_scr=l_scr, bias=bias, natural_rows=natural_rows):
            cur = pl.ds(pl.multiple_of(idx * QBLK, QBLK), QBLK)
            q = qp[cur, :]
            if with_prev:
                keys_t = jnp.concatenate([kt[idx - 1], kt[idx]], axis=1)
                kv_rows = pl.ds(pl.multiple_of((idx - 1) * QBLK, QBLK), 2 * QBLK)
            else:
                keys_t = kt[idx]
                kv_rows = cur
            pvs, ms = [], []
            for hh, v_scr in ((0, va), (1, vb)):
                qh = jnp.where(head0 == (hh == 0), q, jnp.zeros_like(q))
                s = jnp.dot(qh, keys_t, preferred_element_type=F32)
                s = s + (bias[hh] if with_prev else bias[hh][:, QBLK:])
                m = jnp.max(s, axis=-1, keepdims=True)
                p = jnp.exp(s - m).astype(BF16)
                pvs.append(jnp.dot(p, v_scr[kv_rows, :], preferred_element_type=F32))
                ms.append(m)
            num = jnp.where(head0, pvs[0], pvs[1])
            den = pltpu.roll(jnp.where(head0, pvs[1], pvs[0]), HEAD_DIM, axis=1)
            dst = natural_rows(idx)
            o_scr[dst, :] = num / den
            l_scr[dst, :] = jnp.where(head0, ms[0], ms[1]) + jnp.log(den)

        def first_blocks(res, carry, block=block, n_blocks=n_blocks):
            block(res * n_blocks, False)
            return carry

        lax.fori_loop(0, dil, first_blocks, 0, unroll=min(dil, ATTN_UNROLL_FIRST))

        if n_blocks > 1:
            later = n_blocks - 1

            def later_blocks(k, carry, block=block, later=later, n_blocks=n_blocks):
                block((k // later) * n_blocks + k % later + 1, True)
                return carry

            lax.fori_loop(0, dil * later, later_blocks, 0, unroll=ATTN_UNROLL_LATER[dil])

    def merge(k, carry):
        r = pl.ds(pl.multiple_of(k * QBLK, QBLK), QBLK)
        la, lb, lc = l0[r, :], l1[r, :], l2[r, :]
        m = jnp.maximum(jnp.maximum(la, lb), lc)
        wa, wb, wc = jnp.exp(la - m), jnp.exp(lb - m), jnp.exp(lc - m)
        tot = wa + wb + wc
        o_ref[r, :] = ((wa * o0[r, :] + wb * o1[r, :] + wc * o2[r, :]) / tot).astype(BF16)
        return carry

    lax.fori_loop(0, n_qblk, merge, 0, unroll=2)


def _attention(proj, slopes):
    pairs = HEADS_PER_PATTERN // 2

    def col(base, g):
        return lambda b, p: (b, (base + g * HEADS_PER_PATTERN * HEAD_DIM) // LANES + p)

    in_specs = [pl.BlockSpec(memory_space=pltpu.SMEM)]
    for g in range(len(DSWA_PATTERNS)):
        for base in (OFF_Q, OFF_K, OFF_V):
            in_specs.append(pl.BlockSpec((SEQ, LANES), col(base, g)))
    scr = pltpu.VMEM((SEQ, LANES), F32)
    seq_b = pltpu.VMEM((SEQ, LANES), BF16)
    kt_b = pltpu.VMEM((SEQ // QBLK, LANES, QBLK), BF16)
    return pl.pallas_call(
        _attn_kernel,
        out_shape=jax.ShapeDtypeStruct((TOKENS, ATTN_WIDTH), BF16),
        grid=(BATCH, pairs),
        in_specs=in_specs,
        out_specs=pl.BlockSpec((SEQ, LANES), lambda b, p: (b, p)),
        scratch_shapes=[seq_b, kt_b, seq_b, seq_b] + [scr] * 6,
        compiler_params=_cparams(("arbitrary", "arbitrary")),
        name="dilated_attention",
    )(slopes, *([proj] * 9))


N_MERGE_CHUNKS = D_MODEL // TN_MERGE
N_MERGE_TILES = TOKENS // TM_MERGE


def _merge_kernel(*refs):
    (s_ref, a_ref, cb_ref, cc_ref, ch_ref, cch_ref, chh_ref), refs = refs[:7], refs[7:]
    gate_refs, refs = refs[:N_BRANCH * N_MERGE_CHUNKS], refs[N_BRANCH * N_MERGE_CHUNKS:]
    (x_ref, gt_ref, sh2_ref, sc2_ref, gpost_ref, gffn_ref, cw_ref, wss_ref, wat_ref, wcv_ref, wo_ref,
     o_ref, h2_ref, z_scr, m_scr, y_even, y_odd) = refs
    y_scrs = (y_even, y_odd)
    step = pl.program_id(0)
    tiles_per_seq = SEQ // TM_MERGE
    seq_start = (step % tiles_per_seq) == 0
    b_prev = jnp.maximum(step - 1, 0) // tiles_per_seq

    @pl.when(step == 0)
    def _():
        y_odd[...] = jnp.zeros((TM_MERGE, D_MODEL), F32)

    def epilogue(y_scr):
        y = y_scr[...]
        ms = jnp.mean(y * y, axis=-1, keepdims=True)
        yn = y * lax.rsqrt(ms + RMS_EPS) * gpost_ref[...]
        x_new = x_ref[...] + gt_ref[pl.ds(b_prev, 1), :] * yn
        o_ref[...] = x_new
        h2_ref[...] = _norm_mod(x_new, gffn_ref[...], sc2_ref[pl.ds(b_prev, 1), :], sh2_ref[pl.ds(b_prev, 1), :])

    def matmuls(y_scr):
        halo = cch_ref[...] * chh_ref[...]
        z_scr[0:8, :] = jnp.where(seq_start, 0.0, halo)
        z_scr[8:8 + TM_MERGE, :] = cc_ref[...] * ch_ref[...]
        conv = (cw_ref[0:1, :] * z_scr[pl.ds(8, TM_MERGE), :]
                + cw_ref[1:2, :] * z_scr[pl.ds(7, TM_MERGE), :]
                + cw_ref[2:3, :] * z_scr[pl.ds(6, TM_MERGE), :])
        cv = (cb_ref[...] * conv).astype(BF16)
        s = s_ref[...]
        a = a_ref[...]
        for c in range(N_MERGE_CHUNKS):
            cols = slice(c * TN_MERGE, (c + 1) * TN_MERGE)
            y_ssm = jnp.dot(s, wss_ref[:, cols], preferred_element_type=F32)
            y_att = jnp.dot(a, wat_ref[:, cols], preferred_element_type=F32)
            y_cv = jnp.dot(cv, wcv_ref[:, cols], preferred_element_type=F32)
            merged = (gate_refs[c][...] * y_ssm + gate_refs[N_MERGE_CHUNKS + c][...] * y_att
                      + gate_refs[2 * N_MERGE_CHUNKS + c][...] * y_cv)
            m_scr[:, cols] = merged.astype(BF16)
        y_scr[...] = jnp.dot(m_scr[...], wo_ref[...], preferred_element_type=F32)

    for parity in (0, 1):
        @pl.when(jnp.logical_and(step < N_MERGE_TILES, step % 2 == parity))
        def _(parity=parity):
            epilogue(y_scrs[1 - parity])
            matmuls(y_scrs[parity])

    @pl.when(step == N_MERGE_TILES)
    def _():
        epilogue(y_scrs[(N_MERGE_TILES - 1) % 2])


def _merge(l, s, a, proj, x, mod, g_post3, g_ffn3, conv_w, w_ssm_out, w_attn_out, w_conv_out, w_o):
    tm = TM_MERGE
    d = D_MODEL
    cw = CONV_WIDTH
    cb_blk = OFF_CONV // cw
    halo_blocks = tm // 8
    last = N_MERGE_TILES - 1

    def cur(i):
        return jnp.minimum(i, last)

    def halo_map(col_blk):
        return lambda i: (jnp.maximum(cur(i) * halo_blocks - 1, 0), col_blk)

    def gate_spec(branch, chunk):
        blk = (OFF_GATE + branch * d) // TN_MERGE + chunk
        return pl.BlockSpec((tm, TN_MERGE), lambda i: (cur(i), blk))

    gate_specs = [gate_spec(br, c) for br in range(N_BRANCH) for c in range(N_MERGE_CHUNKS)]
    prev_tile = pl.BlockSpec((tm, d), lambda i: (jnp.maximum(i - 1, 0), 0))
    return pl.pallas_call(
        _merge_kernel,
        out_shape=(jax.ShapeDtypeStruct((TOKENS, d), F32), jax.ShapeDtypeStruct((TOKENS, d), BF16)),
        grid=(N_MERGE_TILES + 1,),
        in_specs=[
            pl.BlockSpec((tm, SSM_WIDTH), lambda i: (cur(i) % (SEQ // tm), cur(i) // (SEQ // tm))),
            pl.BlockSpec((tm, ATTN_WIDTH), lambda i: (cur(i), 0)),
            pl.BlockSpec((tm, cw), lambda i: (cur(i), cb_blk)),
            pl.BlockSpec((tm, cw), lambda i: (cur(i), cb_blk + 1)),
            pl.BlockSpec((tm, cw), lambda i: (cur(i), cb_blk + 2)),
            pl.BlockSpec((8, cw), halo_map(cb_blk + 1)),
            pl.BlockSpec((8, cw), halo_map(cb_blk + 2)),
        ] + gate_specs + [
            prev_tile,
            pl.BlockSpec((None, BATCH, d), lambda i: (l, 0, 2)),
            pl.BlockSpec((None, BATCH, d), lambda i: (l, 0, 3)),
            pl.BlockSpec((None, BATCH, d), lambda i: (l, 0, 4)),
            _resident((None, 1, d), lambda i: (l, 0, 0)),
            _resident((None, 1, d), lambda i: (l, 0, 0)),
            _resident((None, 3, cw), lambda i: (l, 0, 0)),
            _resident((None, SSM_WIDTH, d), lambda i: (l, 0, 0)),
            _resident((None, ATTN_WIDTH, d), lambda i: (l, 0, 0)),
            _resident((None, cw, d), lambda i: (l, 0, 0)),
            _resident((None, d, d), lambda i: (l, 0, 0)),
        ],
        out_specs=(prev_tile, prev_tile),
        scratch_shapes=[pltpu.VMEM((tm + 8, cw), F32), pltpu.VMEM((tm, d), BF16),
                        pltpu.VMEM((tm, d), F32), pltpu.VMEM((tm, d), F32)],
        compiler_params=_cparams(("arbitrary",)),
        name="branch_merge",
    )(s, a, *([proj] * (5 + len(gate_specs))), x, mod, mod, mod, g_post3, g_ffn3, conv_w,
      w_ssm_out, w_attn_out, w_conv_out, w_o)


def _up_kernel(h_ref, wa_ref, wb_ref, cwa_ref, cwb_ref, o_ref):
    half = SEQ // 2
    row = lax.broadcasted_iota(jnp.int32, (half, 1), 0)
    for first in (True, False):
        lead = 0 if first else UP_HALO
        h = h_ref[pl.ds(0 if first else half - UP_HALO, half + lead), :]
        convs = []
        for w_ref, cw_ref in ((wa_ref, cwa_ref), (wb_ref, cwb_ref)):
            z = jnp.dot(h, w_ref[...], preferred_element_type=F32)
            z1 = pltpu.roll(z, 1, axis=0)
            z2 = pltpu.roll(z, 2, axis=0)
            if first:
                z1 = jnp.where(row >= 1, z1, 0.0)
                z2 = jnp.where(row >= 2, z2, 0.0)
            y = cw_ref[0:1, :] * z + cw_ref[1:2, :] * z1 + cw_ref[2:3, :] * z2
            convs.append(y[lead:, :])
        ca, cb = convs
        o_ref[pl.ds(0 if first else half, half), :] = (ca * _sigmoid(ca) * cb).astype(BF16)


def _mlp_up(l, h, w_up, conv_w):
    d = D_MODEL
    tf = TF_UP
    nf = D_FF // tf
    return pl.pallas_call(
        _up_kernel,
        out_shape=jax.ShapeDtypeStruct((TOKENS, D_FF), BF16),
        grid=(BATCH, nf),
        in_specs=[
            pl.BlockSpec((SEQ, d), lambda i, j: (i, 0)),
            pl.BlockSpec((None, d, tf), lambda i, j: (l, 0, j)),
            pl.BlockSpec((None, d, tf), lambda i, j: (l, 0, nf + j)),
            pl.BlockSpec((None, 3, tf), lambda i, j: (l, 0, j)),
            pl.BlockSpec((None, 3, tf), lambda i, j: (l, 0, nf + j)),
        ],
        out_specs=pl.BlockSpec((SEQ, tf), lambda i, j: (i, j)),
        compiler_params=_cparams(("arbitrary", "arbitrary")),
        name="mlp_up",
    )(h, w_up, w_up, conv_w, conv_w)


N_DOWN_TILES = TOKENS // TM_DOWN


def _down_kernel(act_ref, w_ref, x_ref, gt_ref, gpost_ref, *rest):
    y_scrs = rest[-2:]
    s = pl.program_id(0)
    b = jnp.maximum(s - 1, 0) // (SEQ // TM_DOWN)

    @pl.when(s == 0)
    def _():
        y_scrs[1][...] = jnp.zeros((TM_DOWN, D_MODEL), F32)

    def epilogue(y_scr):
        y = y_scr[...]
        ms = jnp.mean(y * y, axis=-1, keepdims=True)
        yn = y * lax.rsqrt(ms + RMS_EPS) * gpost_ref[...]
        x_new = x_ref[...] + gt_ref[pl.ds(b, 1), :] * yn
        if len(rest) == 3:
            rest[0][...] = x_new
        else:
            sh_ref, sc_ref, gnext_ref, o_ref, hn_ref = rest[:5]
            o_ref[...] = x_new
            hn_ref[...] = _norm_mod(x_new, gnext_ref[...], sc_ref[pl.ds(b, 1), :], sh_ref[pl.ds(b, 1), :])

    for parity in (0, 1):
        @pl.when(jnp.logical_and(s < N_DOWN_TILES, s % 2 == parity))
        def _(parity=parity):
            epilogue(y_scrs[1 - parity])
            y_scrs[parity][...] = jnp.dot(act_ref[...], w_ref[...], preferred_element_type=F32)

    @pl.when(s == N_DOWN_TILES)
    def _():
        epilogue(y_scrs[(N_DOWN_TILES - 1) % 2])


def _mlp_down(l, act, w_down, x, mod, g_post3, g_mix3):
    d = D_MODEL
    tm = TM_DOWN
    last = l == DEPTH - 1
    cur = pl.BlockSpec((tm, D_FF), lambda s: (jnp.minimum(s, N_DOWN_TILES - 1), 0))
    prev = pl.BlockSpec((tm, d), lambda s: (jnp.maximum(s - 1, 0), 0))
    in_specs = [
        cur,
        _resident((None, D_FF, d), lambda s: (l, 0, 0)),
        prev,
        pl.BlockSpec((None, BATCH, d), lambda s: (l, 0, 5)),
        _resident((None, 1, d), lambda s: (l, 0, 0)),
    ]
    args = [act, w_down, x, mod, g_post3]
    x_shape = jax.ShapeDtypeStruct((TOKENS, d), F32)
    if last:
        out_shape, out_specs = x_shape, prev
    else:
        in_specs += [
            pl.BlockSpec((None, BATCH, d), lambda s: (l + 1, 0, 0)),
            pl.BlockSpec((None, BATCH, d), lambda s: (l + 1, 0, 1)),
            _resident((None, 1, d), lambda s: (l + 1, 0, 0)),
        ]
        args += [mod, mod, g_mix3]
        out_shape, out_specs = (x_shape, jax.ShapeDtypeStruct((TOKENS, d), BF16)), (prev, prev)
    out = pl.pallas_call(
        _down_kernel,
        out_shape=out_shape,
        grid=(N_DOWN_TILES + 1,),
        in_specs=in_specs,
        out_specs=out_specs,
        scratch_shapes=[pltpu.VMEM((tm, d), F32), pltpu.VMEM((tm, d), F32)],
        compiler_params=_cparams(("arbitrary",)),
        name="mlp_down",
    )(*args)
    return (out, None) if last else out


def _alibi_slopes():
    h = N_ATTN_HEADS
    return np.array([2.0 ** (-8.0 * (i + 1) / h) for i in range(h)], dtype=np.float32)


def _time_major_permutation():
    rows = BATCH * SSM_CHUNK
    i = np.arange(rows)
    src = (i % BATCH) * SSM_CHUNK + i // BATCH
    p = np.zeros((rows, rows), np.float32)
    p[i, src] = 1.0
    return p


def kernel(x, c, w_mod, b_mod, g_pre_mix, g_post_mix, g_pre_ffn, g_post_ffn, w_in, ssm_log_dt, ssm_a_re,
           ssm_a_im, ssm_b_re, ssm_b_im, ssm_c_re, ssm_c_im, ssm_d, w_glu, b_glu, conv_mix_w, w_ssm_out,
           w_attn_out, w_conv_out, b_gate, w_o, w_up, ffn_conv_w, w_down):
    w_in_b = w_in.astype(BF16)
    bias3 = _rows3(jnp.concatenate([jnp.zeros((DEPTH, OFF_GATE), F32), b_gate], axis=-1))
    w_glu_b = w_glu.astype(BF16)
    w_ssm_out_b = w_ssm_out.astype(BF16)
    w_attn_out_b = w_attn_out.astype(BF16)
    w_conv_out_b = w_conv_out.astype(BF16)
    w_o_b = w_o.astype(BF16)
    w_up_b = w_up.astype(BF16)
    w_down_b = w_down.astype(BF16)
    g_pre_mix3, g_post_mix3 = _rows3(g_pre_mix), _rows3(g_post_mix)
    g_pre_ffn3, g_post_ffn3 = _rows3(g_pre_ffn), _rows3(g_post_ffn)
    d_skip3, b_glu3 = _rows3(ssm_d), _rows3(b_glu)

    mod = _modulation(c, w_mod, b_mod)
    lr, li, bbr, bbi = _ssm_params(ssm_log_dt, ssm_a_re, ssm_a_im, ssm_b_re, ssm_b_im)
    perm_np = _time_major_permutation()
    perm = jnp.asarray(perm_np, dtype=BF16)
    slopes = jnp.asarray(_alibi_slopes())

    xf = x.reshape(TOKENS, D_MODEL)
    h = _first_norm(xf, g_pre_mix3, mod)
    for l in range(DEPTH):
        proj = _in_projection(l, h, w_in_b, bias3)
        bb, cre, cim, lam = _ssm_layout(lr[l], li[l], bbr[:, l], bbi[:, l], ssm_c_re[l], ssm_c_im[l])
        s = _ssm_mixer(l, proj.reshape(BATCH, SEQ, N_IN), perm, bb, cre, cim, lam,
                       d_skip3, w_glu_b, b_glu3)
        a = _attention(proj, slopes)
        xf, h = _merge(l, s.reshape(SEQ, BATCH * SSM_WIDTH), a, proj, xf, mod, g_post_mix3, g_pre_ffn3, conv_mix_w,
                       w_ssm_out_b, w_attn_out_b, w_conv_out_b, w_o_b)
        act = _mlp_up(l, h, w_up_b, ffn_conv_w)
        xf, h = _mlp_down(l, act, w_down_b, xf, mod, g_post_ffn3, g_pre_mix3)
    return xf.reshape(BATCH, SEQ, D_MODEL)
```

```python
import numpy as np
import jax
import jax.numpy as jnp
from jax import lax
from jax.experimental import pallas as pl
from jax.experimental.pallas import tpu as pltpu

F32 = jnp.float32
BF16 = jnp.bfloat16

D_MODEL = 2048
BATCH = 8
SEQ = 2048
DEPTH = 4
TOKENS = BATCH * SEQ
RMS_EPS = 1e-6
NEG_INF = -1e30
N_BRANCH = 3
SSM_WIDTH = 512
SSM_GROUP = 16
SSM_GROUPS = 32
SSM_STATE = 64
HEAD_DIM = 64
DSWA_PATTERNS = ((128, 1), (512, 4), (2048, 16))
ATTN_WIDTH = 512
HEADS_PER_PATTERN = 8
N_ATTN_HEADS = 24
QKV_WIDTH = 1536
CONV_WIDTH = 512
D_FF = 5632
OFF_U = 0
OFF_Q = SSM_WIDTH
OFF_K = OFF_Q + QKV_WIDTH
OFF_V = OFF_K + QKV_WIDTH
OFF_CONV = OFF_V + QKV_WIDTH
OFF_GATE = OFF_CONV + 3 * CONV_WIDTH
N_IN = OFF_GATE + N_BRANCH * D_MODEL

LANES = 128
VMEM_LIMIT = 62 * 1024 * 1024

TM_NORM = 1024
TN_PROJ = 1280
TF_UP = 512
UP_HALO = 16
UP_SPLIT = 1024
TM_MERGE = 256
TN_MERGE = 512
TM_DOWN = 256
SSM_CHUNK = 128
QBLK = 128


def _cparams(sem):
    return pltpu.CompilerParams(dimension_semantics=sem, vmem_limit_bytes=VMEM_LIMIT)


def _resident(shape, index_map):
    return pl.BlockSpec(shape, index_map, pipeline_mode=pl.Buffered(1))


def _sigmoid(x):
    return 0.5 * jnp.tanh(0.5 * x) + 0.5


def _rows3(a):
    return a.reshape(DEPTH, 1, a.shape[-1])


def _norm_mod(x, g, scale_row, shift_row):
    ms = jnp.mean(x * x, axis=-1, keepdims=True)
    y = x * lax.rsqrt(ms + RMS_EPS) * g
    return (y * (1.0 + scale_row) + shift_row).astype(BF16)


def _mod_kernel(c_ref, w_ref, b_ref, o_ref):
    c = c_ref[...]
    cond = (c * jax.nn.sigmoid(c)).astype(BF16)
    o_ref[0] = jnp.dot(cond, w_ref[0].astype(BF16), preferred_element_type=F32) + b_ref[0]


def _modulation(c, w_mod, b_mod):
    tn = 1024
    n = 6 * D_MODEL
    return pl.pallas_call(
        _mod_kernel,
        out_shape=jax.ShapeDtypeStruct((DEPTH, BATCH, n), F32),
        grid=(DEPTH, n // tn),
        in_specs=[
            pl.BlockSpec((BATCH, D_MODEL), lambda l, j: (0, 0)),
            pl.BlockSpec((1, D_MODEL, tn), lambda l, j: (l, 0, j)),
            pl.BlockSpec((1, 1, tn), lambda l, j: (l, 0, j)),
        ],
        out_specs=pl.BlockSpec((1, BATCH, tn), lambda l, j: (l, 0, j)),
        compiler_params=_cparams(("arbitrary", "arbitrary")),
        name="modulation",
    )(c, w_mod, b_mod.reshape(DEPTH, 1, n))


def _norm_kernel(x_ref, g_ref, sh_ref, sc_ref, o_ref):
    b = pl.program_id(0) // (SEQ // TM_NORM)
    rows = 128

    def body(k, carry):
        r = pl.ds(pl.multiple_of(k * rows, rows), rows)
        o_ref[r, :] = _norm_mod(x_ref[r, :], g_ref[...], sc_ref[pl.ds(b, 1), :], sh_ref[pl.ds(b, 1), :])
        return carry

    lax.fori_loop(0, TM_NORM // rows, body, 0)


def _first_norm(x, g3, mod):
    d = D_MODEL
    return pl.pallas_call(
        _norm_kernel,
        out_shape=jax.ShapeDtypeStruct((TOKENS, d), BF16),
        grid=(TOKENS // TM_NORM,),
        in_specs=[
            pl.BlockSpec((TM_NORM, d), lambda i: (i, 0)),
            pl.BlockSpec((None, 1, d), lambda i: (0, 0, 0)),
            pl.BlockSpec((None, BATCH, d), lambda i: (0, 0, 0)),
            pl.BlockSpec((None, BATCH, d), lambda i: (0, 0, 1)),
        ],
        out_specs=pl.BlockSpec((TM_NORM, d), lambda i: (i, 0)),
        compiler_params=_cparams(("arbitrary",)),
        name="first_norm",
    )(x, g3, mod, mod)


def _inproj_kernel(h_ref, w_ref, bias_ref, o_ref):
    j = pl.program_id(1)
    raw_tiles = OFF_GATE // TN_PROJ
    raw_cols_in_mixed = OFF_GATE - raw_tiles * TN_PROJ

    @pl.when(j < raw_tiles)
    def _():
        o_ref[...] = jnp.dot(h_ref[...], w_ref[...].astype(BF16), preferred_element_type=F32)

    @pl.when(j == raw_tiles)
    def _():
        acc = jnp.dot(h_ref[...], w_ref[...].astype(BF16), preferred_element_type=F32)
        col = lax.broadcasted_iota(jnp.int32, acc.shape, 1)
        o_ref[...] = jnp.where(col < raw_cols_in_mixed, acc, _sigmoid(acc + bias_ref[...]))

    @pl.when(j > raw_tiles)
    def _():
        acc = jnp.dot(h_ref[...], w_ref[...].astype(BF16), preferred_element_type=F32)
        o_ref[...] = _sigmoid(acc + bias_ref[...])


def _in_projection(l, h, w_in, bias3):
    d = D_MODEL
    return pl.pallas_call(
        _inproj_kernel,
        out_shape=jax.ShapeDtypeStruct((TOKENS, N_IN), F32),
        grid=(BATCH, N_IN // TN_PROJ),
        in_specs=[
            pl.BlockSpec((SEQ, d), lambda i, j: (i, 0), pipeline_mode=pl.Buffered(1)),
            pl.BlockSpec((None, d, TN_PROJ), lambda i, j: (l, 0, j)),
            pl.BlockSpec((None, 1, TN_PROJ), lambda i, j: (l, 0, j)),
        ],
        out_specs=pl.BlockSpec((SEQ, TN_PROJ), lambda i, j: (i, j)),
        compiler_params=_cparams(("arbitrary", "arbitrary")),
        name="in_projection",
    )(h, w_in, bias3)


def _ssm_param_kernel(ldt_ref, ar_ref, ai_ref, br_ref, bi_ref, lr_ref, li_ref, bbr_ref, bbi_ref):
    dt = jnp.exp(ldt_ref[...])
    ar = ar_ref[...]
    ai = ai_ref[...]
    mag = jnp.exp(ar * dt)
    lr = mag * jnp.cos(ai * dt)
    li = mag * jnp.sin(ai * dt)
    den = ar * ar + ai * ai
    fr = ((lr - 1.0) * ar + li * ai) / den
    fi = (li * ar - (lr - 1.0) * ai) / den
    lr_ref[...] = lr
    li_ref[...] = li
    for ch in range(SSM_GROUP):
        br = br_ref[ch]
        bi = bi_ref[ch]
        bbr_ref[ch] = fr * br - fi * bi
        bbi_ref[ch] = fr * bi + fi * br


def _ssm_params(log_dt, a_re, a_im, b_re, b_im):
    n = DEPTH * SSM_GROUPS
    mat = jax.ShapeDtypeStruct((n, SSM_STATE), F32)
    cube = jax.ShapeDtypeStruct((SSM_GROUP, n, SSM_STATE), F32)

    def channel_major(b):
        return b.transpose(3, 0, 1, 2).reshape(SSM_GROUP, n, SSM_STATE)

    lr, li, bbr, bbi = pl.pallas_call(
        _ssm_param_kernel,
        out_shape=(mat, mat, cube, cube),
        name="ssm_params",
    )(log_dt.reshape(n, 1), a_re.reshape(n, SSM_STATE), a_im.reshape(n, SSM_STATE),
      channel_major(b_re), channel_major(b_im))
    shape3 = (DEPTH, SSM_GROUPS, SSM_STATE)
    return (lr.reshape(shape3), li.reshape(shape3),
            bbr.reshape((SSM_GROUP,) + shape3), bbi.reshape((SSM_GROUP,) + shape3))


N_SLAB = SSM_WIDTH // LANES
GROUPS_PER_SLAB = LANES // SSM_GROUP
STATES_PER_SLAB = GROUPS_PER_SLAB * SSM_STATE
N_STATES = SSM_GROUPS * SSM_STATE


def _ssm_layout(lr, li, bbr, bbi, c_re, c_im):
    eye = jnp.eye(GROUPS_PER_SLAB, dtype=F32)

    def in_blocks(bb):
        t = bb.reshape(SSM_GROUP, N_SLAB, GROUPS_PER_SLAB, SSM_STATE).transpose(1, 2, 0, 3)
        blk = eye[None, :, None, :, None] * t[:, :, :, None, :]
        return blk.reshape(N_SLAB, LANES, STATES_PER_SLAB)

    def out_blocks(cc):
        t = cc.reshape(N_SLAB, GROUPS_PER_SLAB, SSM_GROUP, SSM_STATE).transpose(0, 1, 3, 2)
        blk = eye[None, :, None, :, None] * t[:, :, :, None, :]
        return blk.reshape(N_SLAB, STATES_PER_SLAB, LANES)

    bb = jnp.concatenate([in_blocks(bbr), in_blocks(bbi)], axis=-1).astype(BF16)
    lam = jnp.stack([lr.reshape(N_STATES), li.reshape(N_STATES)], axis=0)
    return bb, out_blocks(c_re).astype(BF16), out_blocks(c_im).astype(BF16), lam


def _ssm_kernel(u_ref, perm_ref, permt_ref, bb_ref, cre_ref, cim_ref, lam_ref, dsk_ref, wglu_ref, bglu_ref,
                o_ref, x_scr, h_scr):
    rows = BATCH * SSM_CHUNK

    @pl.when(pl.program_id(0) == 0)
    def _():
        h_scr[...] = jnp.zeros_like(h_scr)

    u = u_ref[...].reshape(rows, SSM_WIDTH).astype(BF16)
    u_tm = jnp.dot(perm_ref[...], u, preferred_element_type=F32)
    u_tm_b = u_tm.astype(BF16)

    for s in range(N_SLAB):
        xs = jnp.dot(u_tm_b[:, s * LANES:(s + 1) * LANES], bb_ref[s], preferred_element_type=F32)
        x_scr[:, s * STATES_PER_SLAB:(s + 1) * STATES_PER_SLAB] = xs[:, :STATES_PER_SLAB]
        x_scr[:, N_STATES + s * STATES_PER_SLAB:N_STATES + (s + 1) * STATES_PER_SLAB] = xs[:, STATES_PER_SLAB:]

    width = STATES_PER_SLAB
    for s in range(N_STATES // width):
        re_cols = pl.ds(s * width, width)
        im_cols = pl.ds(N_STATES + s * width, width)
        lr = jnp.broadcast_to(lam_ref[0:1, s * width:(s + 1) * width], (BATCH, width))
        li = jnp.broadcast_to(lam_ref[1:2, s * width:(s + 1) * width], (BATCH, width))

        def step(t, carry, re_cols=re_cols, im_cols=im_cols, lr=lr, li=li):
            hr, hi = carry
            r0 = pl.multiple_of(t * BATCH, BATCH)
            xr = x_scr[pl.ds(r0, BATCH), re_cols]
            xi = x_scr[pl.ds(r0, BATCH), im_cols]
            nr = lr * hr - li * hi + xr
            ni = lr * hi + li * hr + xi
            x_scr[pl.ds(r0, BATCH), re_cols] = nr
            x_scr[pl.ds(r0, BATCH), im_cols] = ni
            return nr, ni

        hr, hi = lax.fori_loop(0, SSM_CHUNK, step, (h_scr[:, re_cols], h_scr[:, im_cols]), unroll=4)
        h_scr[:, re_cols] = hr
        h_scr[:, im_cols] = hi

    ys = []
    for s in range(N_SLAB):
        h_re = x_scr[:, s * STATES_PER_SLAB:(s + 1) * STATES_PER_SLAB].astype(BF16)
        h_im = x_scr[:, N_STATES + s * STATES_PER_SLAB:N_STATES + (s + 1) * STATES_PER_SLAB].astype(BF16)
        ys.append(jnp.dot(h_re, cre_ref[s], preferred_element_type=F32)
                  - jnp.dot(h_im, cim_ref[s], preferred_element_type=F32))
    y = jnp.concatenate(ys, axis=-1) + dsk_ref[...] * u_tm
    g = jax.nn.gelu(y)
    gate = jnp.dot(g.astype(BF16), wglu_ref[...], preferred_element_type=F32) + bglu_ref[...]
    out_tm = (g * _sigmoid(gate)).astype(BF16)
    out = jnp.dot(permt_ref[...], out_tm, preferred_element_type=F32)
    o_ref[...] = out.astype(BF16).reshape(BATCH, SSM_CHUNK, SSM_WIDTH)


def _ssm_mixer(l, proj3, perm, permt, bb, cre, cim, lam, d_skip3, w_glu, b_glu3):
    rows = BATCH * SSM_CHUNK
    return pl.pallas_call(
        _ssm_kernel,
        out_shape=jax.ShapeDtypeStruct((BATCH, SEQ, SSM_WIDTH), BF16),
        grid=(SEQ // SSM_CHUNK,),
        in_specs=[
            pl.BlockSpec((BATCH, SSM_CHUNK, SSM_WIDTH), lambda c: (0, c, OFF_U // SSM_WIDTH)),
            _resident((rows, rows), lambda c: (0, 0)),
            _resident((rows, rows), lambda c: (0, 0)),
            _resident((N_SLAB, LANES, 2 * STATES_PER_SLAB), lambda c: (0, 0, 0)),
            _resident((N_SLAB, STATES_PER_SLAB, LANES), lambda c: (0, 0, 0)),
            _resident((N_SLAB, STATES_PER_SLAB, LANES), lambda c: (0, 0, 0)),
            _resident((2, N_STATES), lambda c: (0, 0)),
            _resident((None, 1, SSM_WIDTH), lambda c: (l, 0, 0)),
            _resident((None, SSM_WIDTH, SSM_WIDTH), lambda c: (l, 0, 0)),
            _resident((None, 1, SSM_WIDTH), lambda c: (l, 0, 0)),
        ],
        out_specs=pl.BlockSpec((BATCH, SSM_CHUNK, SSM_WIDTH), lambda c: (0, c, 0)),
        scratch_shapes=[pltpu.VMEM((rows, 2 * N_STATES), F32), pltpu.VMEM((BATCH, 2 * N_STATES), F32)],
        compiler_params=_cparams(("arbitrary",)),
        name="ssm_mixer",
    )(proj3, perm, permt, bb, cre, cim, lam, d_skip3, w_glu, b_glu3)


ATTN_UNROLL_LATER = {1: 5, 4: 6}
ATTN_UNROLL_FIRST = 8


def _attn_kernel(slopes_ref, q0, k0, v0, q1, k1, v1, q2, k2, v2, o_ref,
                 qp, kt, va, vb, o0, l0, o1, l1, o2, l2):
    pair = pl.program_id(1)
    n_qblk = SEQ // QBLK
    lane = lax.broadcasted_iota(jnp.int32, (QBLK, LANES), 1)
    head0 = lane < HEAD_DIM
    qi = lax.broadcasted_iota(jnp.int32, (QBLK, 2 * QBLK), 0)
    kj = lax.broadcasted_iota(jnp.int32, (QBLK, 2 * QBLK), 1)
    dist = QBLK + qi - kj
    valid = (dist >= 0) & (dist <= QBLK)
    dist = dist.astype(F32)

    groups = ((q0, k0, v0, o0, l0), (q1, k1, v1, o1, l1), (q2, k2, v2, o2, l2))
    for g, (_, dil) in enumerate(DSWA_PATTERNS):
        q_ref, k_ref, v_ref, o_scr, l_scr = groups[g]
        n_blocks = SEQ // (dil * QBLK)

        def rows(start, dil=dil):
            if dil == 1:
                return pl.ds(pl.multiple_of(start, QBLK), QBLK)
            return pl.ds(start, QBLK, stride=dil)

        def natural_rows(idx, rows=rows, n_blocks=n_blocks, dil=dil):
            return rows(dil * QBLK * (idx % n_blocks) + idx // n_blocks)

        def reorder(idx, carry, q_ref=q_ref, k_ref=k_ref, v_ref=v_ref, natural_rows=natural_rows):
            src = natural_rows(idx)
            dst = pl.ds(pl.multiple_of(idx * QBLK, QBLK), QBLK)
            qp[dst, :] = (q_ref[src, :] * (HEAD_DIM ** -0.5)).astype(BF16)
            kt[idx] = k_ref[src, :].T.astype(BF16)
            v = v_ref[src, :]
            va[dst, :] = jnp.where(head0, v, 1.0).astype(BF16)
            vb[dst, :] = jnp.where(head0, 1.0, v).astype(BF16)
            return carry

        lax.fori_loop(0, n_qblk, reorder, 0, unroll=4)

        bias = []
        for hh in range(2):
            slope = slopes_ref[g * HEADS_PER_PATTERN + 2 * pair + hh] * float(dil)
            bias.append(jnp.where(valid, -slope * dist, NEG_INF))

        def block(idx, with_prev, o_scr=o_scr, l_scr=l_scr, bias=bias, natural_rows=natural_rows):
            cur = pl.ds(pl.multiple_of(idx * QBLK, QBLK), QBLK)
            q = qp[cur, :]
            if with_prev:
                keys_t = jnp.concatenate([kt[idx - 1], kt[idx]], axis=1)
                kv_rows = pl.ds(pl.multiple_of((idx - 1) * QBLK, QBLK), 2 * QBLK)
            else:
                keys_t = kt[idx]
                kv_rows = cur
            pvs, ms = [], []
            for hh, v_scr in ((0, va), (1, vb)):
                qh = jnp.where(head0 == (hh == 0), q, jnp.zeros_like(q))
                s = jnp.dot(qh, keys_t, preferred_element_type=F32)
                s = s + (bias[hh] if with_prev else bias[hh][:, QBLK:])
                m = jnp.max(s, axis=-1, keepdims=True)
                p = jnp.exp(s - m).astype(BF16)
                pvs.append(jnp.dot(p, v_scr[kv_rows, :], preferred_element_type=F32))
                ms.append(m)
            num = jnp.where(head0, pvs[0], pvs[1])
            den = pltpu.roll(jnp.where(head0, pvs[1], pvs[0]), HEAD_DIM, axis=1)
            dst = natural_rows(idx)
            o_scr[dst, :] = num / den
            l_scr[dst, :] = jnp.where(head0, ms[0], ms[1]) + jnp.log(den)

        def first_blocks(res, carry, block=block, n_blocks=n_blocks):
            block(res * n_blocks, False)
            return carry

        lax.fori_loop(0, dil, first_blocks, 0, unroll=min(dil, ATTN_UNROLL_FIRST))

        if n_blocks > 1:
            later = n_blocks - 1

            def later_blocks(k, carry, block=block, later=later, n_blocks=n_blocks):
                block((k // later) * n_blocks + k % later + 1, True)
                return carry

            lax.fori_loop(0, dil * later, later_blocks, 0, unroll=ATTN_UNROLL_LATER[dil])

    def merge(k, carry):
        r = pl.ds(pl.multiple_of(k * QBLK, QBLK), QBLK)
        la, lb, lc = l0[r, :], l1[r, :], l2[r, :]
        m = jnp.maximum(jnp.maximum(la, lb), lc)
        wa, wb, wc = jnp.exp(la - m), jnp.exp(lb - m), jnp.exp(lc - m)
        tot = wa + wb + wc
        o_ref[r, :] = ((wa * o0[r, :] + wb * o1[r, :] + wc * o2[r, :]) / tot).astype(BF16)
        return carry

    lax.fori_loop(0, n_qblk, merge, 0, unroll=2)


def _attention(proj, slopes):
    pairs = HEADS_PER_PATTERN // 2

    def col(base, g):
        return lambda b, p: (b, (base + g * HEADS_PER_PATTERN * HEAD_DIM) // LANES + p)

    in_specs = [pl.BlockSpec(memory_space=pltpu.SMEM)]
    for g in range(len(DSWA_PATTERNS)):
        for base in (OFF_Q, OFF_K, OFF_V):
            in_specs.append(pl.BlockSpec((SEQ, LANES), col(base, g)))
    scr = pltpu.VMEM((SEQ, LANES), F32)
    seq_b = pltpu.VMEM((SEQ, LANES), BF16)
    kt_b = pltpu.VMEM((SEQ // QBLK, LANES, QBLK), BF16)
    return pl.pallas_call(
        _attn_kernel,
        out_shape=jax.ShapeDtypeStruct((TOKENS, ATTN_WIDTH), BF16),
        grid=(BATCH, pairs),
        in_specs=in_specs,
        out_specs=pl.BlockSpec((SEQ, LANES), lambda b, p: (b, p)),
        scratch_shapes=[seq_b, kt_b, seq_b, seq_b] + [scr] * 6,
        compiler_params=_cparams(("arbitrary", "arbitrary")),
        name="dilated_attention",
    )(slopes, *([proj] * 9))


N_MERGE_CHUNKS = D_MODEL // TN_MERGE


def _merge_kernel(*refs):
    (s_ref, a_ref, cb_ref, cc_ref, ch_ref, cch_ref, chh_ref), refs = refs[:7], refs[7:]
    gate_refs, refs = refs[:N_BRANCH * N_MERGE_CHUNKS], refs[N_BRANCH * N_MERGE_CHUNKS:]
    (x_ref, gt_ref, sh2_ref, sc2_ref, gpost_ref, gffn_ref, cw_ref, wss_ref, wat_ref, wcv_ref, wo_ref,
     o_ref, h2_ref, z_scr, m_scr) = refs
    i = pl.program_id(0)
    tiles_per_seq = SEQ // TM_MERGE
    b = i // tiles_per_seq
    seq_start = (i % tiles_per_seq) == 0

    halo = cch_ref[...] * chh_ref[...]
    z_scr[0:8, :] = jnp.where(seq_start, 0.0, halo)
    z_scr[8:8 + TM_MERGE, :] = cc_ref[...] * ch_ref[...]
    conv = (cw_ref[0:1, :] * z_scr[pl.ds(8, TM_MERGE), :]
            + cw_ref[1:2, :] * z_scr[pl.ds(7, TM_MERGE), :]
            + cw_ref[2:3, :] * z_scr[pl.ds(6, TM_MERGE), :])
    cv = (cb_ref[...] * conv).astype(BF16)

    s = s_ref[...]
    a = a_ref[...]
    for c in range(N_MERGE_CHUNKS):
        cols = slice(c * TN_MERGE, (c + 1) * TN_MERGE)
        y_ssm = jnp.dot(s, wss_ref[:, cols], preferred_element_type=F32)
        y_att = jnp.dot(a, wat_ref[:, cols], preferred_element_type=F32)
        y_cv = jnp.dot(cv, wcv_ref[:, cols], preferred_element_type=F32)
        merged = (gate_refs[c][...] * y_ssm + gate_refs[N_MERGE_CHUNKS + c][...] * y_att
                  + gate_refs[2 * N_MERGE_CHUNKS + c][...] * y_cv)
        m_scr[:, cols] = merged.astype(BF16)

    y = jnp.dot(m_scr[...], wo_ref[...], preferred_element_type=F32)
    ms = jnp.mean(y * y, axis=-1, keepdims=True)
    yn = y * lax.rsqrt(ms + RMS_EPS) * gpost_ref[...]
    x_new = x_ref[...] + gt_ref[pl.ds(b, 1), :] * yn
    o_ref[...] = x_new
    h2_ref[...] = _norm_mod(x_new, gffn_ref[...], sc2_ref[pl.ds(b, 1), :], sh2_ref[pl.ds(b, 1), :])


def _merge(l, s, a, proj, x, mod, g_post3, g_ffn3, conv_w, w_ssm_out, w_attn_out, w_conv_out, w_o):
    tm = TM_MERGE
    d = D_MODEL
    cw = CONV_WIDTH
    cb_blk = OFF_CONV // cw
    halo_blocks = tm // 8

    def halo_map(col_blk):
        return lambda i: (jnp.maximum(i * halo_blocks - 1, 0), col_blk)

    def gate_spec(branch, chunk):
        blk = (OFF_GATE + branch * d) // TN_MERGE + chunk
        return pl.BlockSpec((tm, TN_MERGE), lambda i: (i, blk))

    gate_specs = [gate_spec(br, c) for br in range(N_BRANCH) for c in range(N_MERGE_CHUNKS)]
    tile = pl.BlockSpec((tm, d), lambda i: (i, 0))
    return pl.pallas_call(
        _merge_kernel,
        out_shape=(jax.ShapeDtypeStruct((TOKENS, d), F32), jax.ShapeDtypeStruct((TOKENS, d), BF16)),
        grid=(TOKENS // tm,),
        in_specs=[
            pl.BlockSpec((tm, SSM_WIDTH), lambda i: (i, 0)),
            pl.BlockSpec((tm, ATTN_WIDTH), lambda i: (i, 0)),
            pl.BlockSpec((tm, cw), lambda i: (i, cb_blk)),
            pl.BlockSpec((tm, cw), lambda i: (i, cb_blk + 1)),
            pl.BlockSpec((tm, cw), lambda i: (i, cb_blk + 2)),
            pl.BlockSpec((8, cw), halo_map(cb_blk + 1)),
            pl.BlockSpec((8, cw), halo_map(cb_blk + 2)),
        ] + gate_specs + [
            tile,
            pl.BlockSpec((None, BATCH, d), lambda i: (l, 0, 2)),
            pl.BlockSpec((None, BATCH, d), lambda i: (l, 0, 3)),
            pl.BlockSpec((None, BATCH, d), lambda i: (l, 0, 4)),
            _resident((None, 1, d), lambda i: (l, 0, 0)),
            _resident((None, 1, d), lambda i: (l, 0, 0)),
            _resident((None, 3, cw), lambda i: (l, 0, 0)),
            _resident((None, SSM_WIDTH, d), lambda i: (l, 0, 0)),
            _resident((None, ATTN_WIDTH, d), lambda i: (l, 0, 0)),
            _resident((None, cw, d), lambda i: (l, 0, 0)),
            _resident((None, d, d), lambda i: (l, 0, 0)),
        ],
        out_specs=(tile, tile),
        scratch_shapes=[pltpu.VMEM((tm + 8, cw), F32), pltpu.VMEM((tm, d), BF16)],
        compiler_params=_cparams(("arbitrary",)),
        name="branch_merge",
    )(s, a, *([proj] * (5 + len(gate_specs))), x, mod, mod, mod, g_post3, g_ffn3, conv_w,
      w_ssm_out, w_attn_out, w_conv_out, w_o)


def _up_kernel(h_ref, wa_ref, wb_ref, cwa_ref, cwb_ref, o_ref):
    wa = wa_ref[...]
    wb = wb_ref[...]
    cwa = cwa_ref[...]
    cwb_half = 0.5 * cwb_ref[...]
    row = lax.broadcasted_iota(jnp.int32, (UP_SPLIT, 1), 0)
    for first in (True, False):
        lead = 0 if first else UP_HALO
        start = 0 if first else UP_SPLIT
        n_rows = UP_SPLIT if first else SEQ - UP_SPLIT
        h = h_ref[pl.ds(start - lead, n_rows + lead), :]
        convs = []
        for w, cw in ((wa, cwa), (wb, cwb_half)):
            z = jnp.dot(h, w, preferred_element_type=F32)
            z1 = pltpu.roll(z, 1, axis=0)
            z2 = pltpu.roll(z, 2, axis=0)
            if first:
                z1 = jnp.where(row >= 1, z1, 0.0)
                z2 = jnp.where(row >= 2, z2, 0.0)
            y = cw[0:1, :] * z + cw[1:2, :] * z1 + cw[2:3, :] * z2
            convs.append(y[lead:, :])
        ca, cb_half = convs
        ab = ca * cb_half
        o_ref[pl.ds(start, n_rows), :] = (ab * jnp.tanh(0.5 * ca) + ab).astype(BF16)


def _mlp_up(l, h, w_up, conv_w):
    d = D_MODEL
    tf = TF_UP
    nf = D_FF // tf
    return pl.pallas_call(
        _up_kernel,
        out_shape=jax.ShapeDtypeStruct((TOKENS, D_FF), BF16),
        grid=(BATCH, nf),
        in_specs=[
            pl.BlockSpec((SEQ, d), lambda i, j: (i, 0)),
            pl.BlockSpec((None, d, tf), lambda i, j: (l, 0, j)),
            pl.BlockSpec((None, d, tf), lambda i, j: (l, 0, nf + j)),
            pl.BlockSpec((None, 3, tf), lambda i, j: (l, 0, j)),
            pl.BlockSpec((None, 3, tf), lambda i, j: (l, 0, nf + j)),
        ],
        out_specs=pl.BlockSpec((SEQ, tf), lambda i, j: (i, j)),
        compiler_params=_cparams(("arbitrary", "arbitrary")),
        name="mlp_up",
    )(h, w_up, w_up, conv_w, conv_w)


def _down_kernel(act_ref, w_ref, x_ref, gt_ref, gpost_ref, *rest):
    b = pl.program_id(0) // (SEQ // TM_DOWN)
    y = jnp.dot(act_ref[...], w_ref[...], preferred_element_type=F32)
    ms = jnp.mean(y * y, axis=-1, keepdims=True)
    yn = y * lax.rsqrt(ms + RMS_EPS) * gpost_ref[...]
    x_new = x_ref[...] + gt_ref[pl.ds(b, 1), :] * yn
    if len(rest) == 1:
        (o_ref,) = rest
        o_ref[...] = x_new
    else:
        sh_ref, sc_ref, gnext_ref, o_ref, hn_ref = rest
        o_ref[...] = x_new
        hn_ref[...] = _norm_mod(x_new, gnext_ref[...], sc_ref[pl.ds(b, 1), :], sh_ref[pl.ds(b, 1), :])


def _mlp_down(l, act, w_down, x, mod, g_post3, g_mix3):
    d = D_MODEL
    tm = TM_DOWN
    last = l == DEPTH - 1
    tile = pl.BlockSpec((tm, d), lambda i: (i, 0))
    in_specs = [
        pl.BlockSpec((tm, D_FF), lambda i: (i, 0)),
        _resident((None, D_FF, d), lambda i: (l, 0, 0)),
        tile,
        pl.BlockSpec((None, BATCH, d), lambda i: (l, 0, 5)),
        _resident((None, 1, d), lambda i: (l, 0, 0)),
    ]
    args = [act, w_down, x, mod, g_post3]
    x_shape = jax.ShapeDtypeStruct((TOKENS, d), F32)
    if last:
        out_shape, out_specs = x_shape, tile
    else:
        in_specs += [
            pl.BlockSpec((None, BATCH, d), lambda i: (l + 1, 0, 0)),
            pl.BlockSpec((None, BATCH, d), lambda i: (l + 1, 0, 1)),
            _resident((None, 1, d), lambda i: (l + 1, 0, 0)),
        ]
        args += [mod, mod, g_mix3]
        out_shape, out_specs = (x_shape, jax.ShapeDtypeStruct((TOKENS, d), BF16)), (tile, tile)
    out = pl.pallas_call(
        _down_kernel,
        out_shape=out_shape,
        grid=(TOKENS // tm,),
        in_specs=in_specs,
        out_specs=out_specs,
        compiler_params=_cparams(("arbitrary",)),
        name="mlp_down",
    )(*args)
    return (out, None) if last else out


def _alibi_slopes():
    h = N_ATTN_HEADS
    return np.array([2.0 ** (-8.0 * (i + 1) / h) for i in range(h)], dtype=np.float32)


def _time_major_permutation():
    rows = BATCH * SSM_CHUNK
    i = np.arange(rows)
    src = (i % BATCH) * SSM_CHUNK + i // BATCH
    p = np.zeros((rows, rows), np.float32)
    p[i, src] = 1.0
    return p


def kernel(x, c, w_mod, b_mod, g_pre_mix, g_post_mix, g_pre_ffn, g_post_ffn, w_in, ssm_log_dt, ssm_a_re,
           ssm_a_im, ssm_b_re, ssm_b_im, ssm_c_re, ssm_c_im, ssm_d, w_glu, b_glu, conv_mix_w, w_ssm_out,
           w_attn_out, w_conv_out, b_gate, w_o, w_up, ffn_conv_w, w_down):
    bias3 = _rows3(jnp.concatenate([jnp.zeros((DEPTH, OFF_GATE), F32), b_gate], axis=-1))
    w_glu_b = w_glu.astype(BF16)
    w_ssm_out_b = w_ssm_out.astype(BF16)
    w_attn_out_b = w_attn_out.astype(BF16)
    w_conv_out_b = w_conv_out.astype(BF16)
    w_o_b = w_o.astype(BF16)
    w_down_b = w_down.astype(BF16)
    w_up_b = w_up.astype(BF16)
    g_pre_mix3, g_post_mix3 = _rows3(g_pre_mix), _rows3(g_post_mix)
    g_pre_ffn3, g_post_ffn3 = _rows3(g_pre_ffn), _rows3(g_post_ffn)
    d_skip3, b_glu3 = _rows3(ssm_d), _rows3(b_glu)

    mod = _modulation(c, w_mod, b_mod)
    lr, li, bbr, bbi = _ssm_params(ssm_log_dt, ssm_a_re, ssm_a_im, ssm_b_re, ssm_b_im)
    perm_np = _time_major_permutation()
    perm = jnp.asarray(perm_np, dtype=BF16)
    permt = jnp.asarray(perm_np.T, dtype=BF16)
    slopes = jnp.asarray(_alibi_slopes())

    xf = x.reshape(TOKENS, D_MODEL)
    h = _first_norm(xf, g_pre_mix3, mod)
    for l in range(DEPTH):
        proj = _in_projection(l, h, w_in, bias3)
        bb, cre, cim, lam = _ssm_layout(lr[l], li[l], bbr[:, l], bbi[:, l], ssm_c_re[l], ssm_c_im[l])
        s = _ssm_mixer(l, proj.reshape(BATCH, SEQ, N_IN), perm, permt, bb, cre, cim, lam,
                       d_skip3, w_glu_b, b_glu3)
        a = _attention(proj, slopes)
        xf, h = _merge(l, s.reshape(TOKENS, SSM_WIDTH), a, proj, xf, mod, g_post_mix3, g_pre_ffn3, conv_mix_w,
                       w_ssm_out_b, w_attn_out_b, w_conv_out_b, w_o_b)
        act = _mlp_up(l, h, w_up_b, ffn_conv_w)
        xf, h = _mlp_down(l, act, w_down_b, xf, mod, g_post_ffn3, g_pre_mix3)
    return xf.reshape(BATCH, SEQ, D_MODEL)
```

```python
import numpy as np
import jax
import jax.numpy as jnp
from jax import lax
from jax.experimental import pallas as pl
from jax.experimental.pallas import tpu as pltpu

F32 = jnp.float32
BF16 = jnp.bfloat16

D_MODEL = 2048
BATCH = 8
SEQ = 2048
DEPTH = 4
TOKENS = BATCH * SEQ
RMS_EPS = 1e-6
NEG_INF = -1e30
N_BRANCH = 3
SSM_WIDTH = 512
SSM_GROUP = 16
SSM_GROUPS = 32
SSM_STATE = 64
HEAD_DIM = 64
DSWA_PATTERNS = ((128, 1), (512, 4), (2048, 16))
ATTN_WIDTH = 512
HEADS_PER_PATTERN = 8
N_ATTN_HEADS = 24
QKV_WIDTH = 1536
CONV_WIDTH = 512
D_FF = 5632
OFF_U = 0
OFF_Q = SSM_WIDTH
OFF_K = OFF_Q + QKV_WIDTH
OFF_V = OFF_K + QKV_WIDTH
OFF_CONV = OFF_V + QKV_WIDTH
OFF_GATE = OFF_CONV + 3 * CONV_WIDTH
N_IN = OFF_GATE + N_BRANCH * D_MODEL

LANES = 128
VMEM_LIMIT = 60 * 1024 * 1024

TM_NORM = 1024
TN_PROJ = 1280
TF_UP = 512
UP_HALO = 16
TM_MERGE = 256
CONV_HALO = 16
TN_MERGE = 512
TM_DOWN = 256
SSM_CHUNK = 128
QBLK = 128


def _cparams(sem):
    return pltpu.CompilerParams(dimension_semantics=sem, vmem_limit_bytes=VMEM_LIMIT)


def _resident(shape, index_map):
    return pl.BlockSpec(shape, index_map, pipeline_mode=pl.Buffered(1))


def _sigmoid(x):
    return 0.5 * jnp.tanh(0.5 * x) + 0.5


def _rows3(a):
    return a.reshape(DEPTH, 1, a.shape[-1])


def _norm_mod(x, g, scale_row, shift_row):
    ms = jnp.mean(x * x, axis=-1, keepdims=True)
    y = x * lax.rsqrt(ms + RMS_EPS) * g
    return (y * (1.0 + scale_row) + shift_row).astype(BF16)


def _mod_kernel(c_ref, w_ref, b_ref, o_ref):
    c = c_ref[...]
    cond = (c * jax.nn.sigmoid(c)).astype(BF16)
    o_ref[0] = jnp.dot(cond, w_ref[0].astype(BF16), preferred_element_type=F32) + b_ref[0]


def _modulation(c, w_mod, b_mod):
    tn = 1024
    n = 6 * D_MODEL
    return pl.pallas_call(
        _mod_kernel,
        out_shape=jax.ShapeDtypeStruct((DEPTH, BATCH, n), F32),
        grid=(DEPTH, n // tn),
        in_specs=[
            pl.BlockSpec((BATCH, D_MODEL), lambda l, j: (0, 0)),
            pl.BlockSpec((1, D_MODEL, tn), lambda l, j: (l, 0, j)),
            pl.BlockSpec((1, 1, tn), lambda l, j: (l, 0, j)),
        ],
        out_specs=pl.BlockSpec((1, BATCH, tn), lambda l, j: (l, 0, j)),
        compiler_params=_cparams(("arbitrary", "arbitrary")),
        name="modulation",
    )(c, w_mod, b_mod.reshape(DEPTH, 1, n))


def _norm_kernel(x_ref, g_ref, sh_ref, sc_ref, o_ref):
    b = pl.program_id(0) // (SEQ // TM_NORM)
    rows = 128

    def body(k, carry):
        r = pl.ds(pl.multiple_of(k * rows, rows), rows)
        o_ref[r, :] = _norm_mod(x_ref[r, :], g_ref[...], sc_ref[pl.ds(b, 1), :], sh_ref[pl.ds(b, 1), :])
        return carry

    lax.fori_loop(0, TM_NORM // rows, body, 0)


def _first_norm(x, g3, mod):
    d = D_MODEL
    return pl.pallas_call(
        _norm_kernel,
        out_shape=jax.ShapeDtypeStruct((TOKENS, d), BF16),
        grid=(TOKENS // TM_NORM,),
        in_specs=[
            pl.BlockSpec((TM_NORM, d), lambda i: (i, 0)),
            pl.BlockSpec((None, 1, d), lambda i: (0, 0, 0)),
            pl.BlockSpec((None, BATCH, d), lambda i: (0, 0, 0)),
            pl.BlockSpec((None, BATCH, d), lambda i: (0, 0, 1)),
        ],
        out_specs=pl.BlockSpec((TM_NORM, d), lambda i: (i, 0)),
        compiler_params=_cparams(("arbitrary",)),
        name="first_norm",
    )(x, g3, mod, mod)


def _inproj_kernel(h_ref, w_ref, bias_ref, o_ref):
    j = pl.program_id(1)
    raw_tiles = OFF_GATE // TN_PROJ
    raw_cols_in_mixed = OFF_GATE - raw_tiles * TN_PROJ

    @pl.when(j < raw_tiles)
    def _():
        o_ref[...] = jnp.dot(h_ref[...], w_ref[...], preferred_element_type=F32).astype(BF16)

    @pl.when(j == raw_tiles)
    def _():
        acc = jnp.dot(h_ref[...], w_ref[...], preferred_element_type=F32)
        col = lax.broadcasted_iota(jnp.int32, acc.shape, 1)
        o_ref[...] = jnp.where(col < raw_cols_in_mixed, acc, _sigmoid(acc + bias_ref[...])).astype(BF16)

    @pl.when(j > raw_tiles)
    def _():
        acc = jnp.dot(h_ref[...], w_ref[...], preferred_element_type=F32)
        o_ref[...] = _sigmoid(acc + bias_ref[...]).astype(BF16)


def _in_projection(l, h, w_in, bias3):
    d = D_MODEL
    return pl.pallas_call(
        _inproj_kernel,
        out_shape=jax.ShapeDtypeStruct((TOKENS, N_IN), BF16),
        grid=(BATCH, N_IN // TN_PROJ),
        in_specs=[
            pl.BlockSpec((SEQ, d), lambda i, j: (i, 0)),
            pl.BlockSpec((None, d, TN_PROJ), lambda i, j: (l, 0, j)),
            pl.BlockSpec((None, 1, TN_PROJ), lambda i, j: (l, 0, j)),
        ],
        out_specs=pl.BlockSpec((SEQ, TN_PROJ), lambda i, j: (i, j)),
        compiler_params=_cparams(("arbitrary", "arbitrary")),
        name="in_projection",
    )(h, w_in, bias3)


def _ssm_param_kernel(ldt_ref, ar_ref, ai_ref, br_ref, bi_ref, lr_ref, li_ref, bbr_ref, bbi_ref):
    dt = jnp.exp(ldt_ref[...])
    ar = ar_ref[...]
    ai = ai_ref[...]
    mag = jnp.exp(ar * dt)
    lr = mag * jnp.cos(ai * dt)
    li = mag * jnp.sin(ai * dt)
    den = ar * ar + ai * ai
    fr = ((lr - 1.0) * ar + li * ai) / den
    fi = (li * ar - (lr - 1.0) * ai) / den
    lr_ref[...] = lr
    li_ref[...] = li
    for ch in range(SSM_GROUP):
        br = br_ref[ch]
        bi = bi_ref[ch]
        bbr_ref[ch] = fr * br - fi * bi
        bbi_ref[ch] = fr * bi + fi * br


def _ssm_params(log_dt, a_re, a_im, b_re, b_im):
    n = DEPTH * SSM_GROUPS
    mat = jax.ShapeDtypeStruct((n, SSM_STATE), F32)
    cube = jax.ShapeDtypeStruct((SSM_GROUP, n, SSM_STATE), F32)

    def channel_major(b):
        return b.transpose(3, 0, 1, 2).reshape(SSM_GROUP, n, SSM_STATE)

    lr, li, bbr, bbi = pl.pallas_call(
        _ssm_param_kernel,
        out_shape=(mat, mat, cube, cube),
        name="ssm_params",
    )(log_dt.reshape(n, 1), a_re.reshape(n, SSM_STATE), a_im.reshape(n, SSM_STATE),
      channel_major(b_re), channel_major(b_im))
    shape3 = (DEPTH, SSM_GROUPS, SSM_STATE)
    return (lr.reshape(shape3), li.reshape(shape3),
            bbr.reshape((SSM_GROUP,) + shape3), bbi.reshape((SSM_GROUP,) + shape3))


N_SLAB = SSM_WIDTH // LANES
GROUPS_PER_SLAB = LANES // SSM_GROUP
STATES_PER_SLAB = GROUPS_PER_SLAB * SSM_STATE
N_STATES = SSM_GROUPS * SSM_STATE


def _ssm_layout(lr, li, bbr, bbi, c_re, c_im):
    eye = jnp.eye(GROUPS_PER_SLAB, dtype=F32)

    def in_blocks(bb):
        t = bb.reshape(SSM_GROUP, N_SLAB, GROUPS_PER_SLAB, SSM_STATE).transpose(1, 2, 0, 3)
        blk = eye[None, :, None, :, None] * t[:, :, :, None, :]
        return blk.reshape(N_SLAB, LANES, STATES_PER_SLAB)

    def out_blocks(cc):
        t = cc.reshape(N_SLAB, GROUPS_PER_SLAB, SSM_GROUP, SSM_STATE).transpose(0, 1, 3, 2)
        blk = eye[None, :, None, :, None] * t[:, :, :, None, :]
        return blk.reshape(N_SLAB, STATES_PER_SLAB, LANES)

    bb = jnp.concatenate([in_blocks(bbr), in_blocks(bbi)], axis=-1).astype(BF16)
    lam = jnp.stack([lr.reshape(N_STATES), li.reshape(N_STATES)], axis=0)
    return bb, out_blocks(c_re).astype(BF16), out_blocks(c_im).astype(BF16), lam


def _ssm_kernel(u_ref, perm_ref, permt_ref, bb_ref, cre_ref, cim_ref, lam_ref, dsk_ref, wglu_ref, bglu_ref,
                o_ref, x_scr, h_scr):
    rows = BATCH * SSM_CHUNK

    @pl.when(pl.program_id(0) == 0)
    def _():
        h_scr[...] = jnp.zeros_like(h_scr)

    u = u_ref[...].reshape(rows, SSM_WIDTH)
    u_tm = jnp.dot(perm_ref[...], u, preferred_element_type=F32)
    u_tm_b = u_tm.astype(BF16)

    for s in range(N_SLAB):
        xs = jnp.dot(u_tm_b[:, s * LANES:(s + 1) * LANES], bb_ref[s], preferred_element_type=F32)
        x_scr[:, s * STATES_PER_SLAB:(s + 1) * STATES_PER_SLAB] = xs[:, :STATES_PER_SLAB]
        x_scr[:, N_STATES + s * STATES_PER_SLAB:N_STATES + (s + 1) * STATES_PER_SLAB] = xs[:, STATES_PER_SLAB:]

    width = STATES_PER_SLAB
    for s in range(N_STATES // width):
        re_cols = pl.ds(s * width, width)
        im_cols = pl.ds(N_STATES + s * width, width)
        lr = jnp.broadcast_to(lam_ref[0:1, s * width:(s + 1) * width], (BATCH, width))
        li = jnp.broadcast_to(lam_ref[1:2, s * width:(s + 1) * width], (BATCH, width))

        def step(t, carry, re_cols=re_cols, im_cols=im_cols, lr=lr, li=li):
            hr, hi = carry
            r0 = pl.multiple_of(t * BATCH, BATCH)
            xr = x_scr[pl.ds(r0, BATCH), re_cols]
            xi = x_scr[pl.ds(r0, BATCH), im_cols]
            nr = lr * hr - li * hi + xr
            ni = lr * hi + li * hr + xi
            x_scr[pl.ds(r0, BATCH), re_cols] = nr
            x_scr[pl.ds(r0, BATCH), im_cols] = ni
            return nr, ni

        hr, hi = lax.fori_loop(0, SSM_CHUNK, step, (h_scr[:, re_cols], h_scr[:, im_cols]), unroll=4)
        h_scr[:, re_cols] = hr
        h_scr[:, im_cols] = hi

    ys = []
    for s in range(N_SLAB):
        h_re = x_scr[:, s * STATES_PER_SLAB:(s + 1) * STATES_PER_SLAB].astype(BF16)
        h_im = x_scr[:, N_STATES + s * STATES_PER_SLAB:N_STATES + (s + 1) * STATES_PER_SLAB].astype(BF16)
        ys.append(jnp.dot(h_re, cre_ref[s], preferred_element_type=F32)
                  - jnp.dot(h_im, cim_ref[s], preferred_element_type=F32))
    y = jnp.concatenate(ys, axis=-1) + dsk_ref[...] * u_tm
    g = jax.nn.gelu(y)
    gate = jnp.dot(g.astype(BF16), wglu_ref[...], preferred_element_type=F32) + bglu_ref[...]
    out_tm = (g * _sigmoid(gate)).astype(BF16)
    out = jnp.dot(permt_ref[...], out_tm, preferred_element_type=F32)
    o_ref[...] = out.astype(BF16).reshape(BATCH, SSM_CHUNK, SSM_WIDTH)


def _ssm_mixer(l, proj3, perm, permt, bb, cre, cim, lam, d_skip3, w_glu, b_glu3):
    rows = BATCH * SSM_CHUNK
    return pl.pallas_call(
        _ssm_kernel,
        out_shape=jax.ShapeDtypeStruct((BATCH, SEQ, SSM_WIDTH), BF16),
        grid=(SEQ // SSM_CHUNK,),
        in_specs=[
            pl.BlockSpec((BATCH, SSM_CHUNK, SSM_WIDTH), lambda c: (0, c, OFF_U // SSM_WIDTH)),
            _resident((rows, rows), lambda c: (0, 0)),
            _resident((rows, rows), lambda c: (0, 0)),
            _resident((N_SLAB, LANES, 2 * STATES_PER_SLAB), lambda c: (0, 0, 0)),
            _resident((N_SLAB, STATES_PER_SLAB, LANES), lambda c: (0, 0, 0)),
            _resident((N_SLAB, STATES_PER_SLAB, LANES), lambda c: (0, 0, 0)),
            _resident((2, N_STATES), lambda c: (0, 0)),
            _resident((None, 1, SSM_WIDTH), lambda c: (l, 0, 0)),
            _resident((None, SSM_WIDTH, SSM_WIDTH), lambda c: (l, 0, 0)),
            _resident((None, 1, SSM_WIDTH), lambda c: (l, 0, 0)),
        ],
        out_specs=pl.BlockSpec((BATCH, SSM_CHUNK, SSM_WIDTH), lambda c: (0, c, 0)),
        scratch_shapes=[pltpu.VMEM((rows, 2 * N_STATES), F32), pltpu.VMEM((BATCH, 2 * N_STATES), F32)],
        compiler_params=_cparams(("arbitrary",)),
        name="ssm_mixer",
    )(proj3, perm, permt, bb, cre, cim, lam, d_skip3, w_glu, b_glu3)


ATTN_UNROLL_LATER = {1: 5, 4: 6}
ATTN_UNROLL_FIRST = 8


def _attn_kernel(slopes_ref, q0, k0, v0, q1, k1, v1, q2, k2, v2, o_ref,
                 qp, kt, va, vb, o0, l0, o1, l1, o2, l2):
    pair = pl.program_id(1)
    n_qblk = SEQ // QBLK
    lane = lax.broadcasted_iota(jnp.int32, (QBLK, LANES), 1)
    head0 = lane < HEAD_DIM
    qi = lax.broadcasted_iota(jnp.int32, (QBLK, 2 * QBLK), 0)
    kj = lax.broadcasted_iota(jnp.int32, (QBLK, 2 * QBLK), 1)
    dist = QBLK + qi - kj
    valid = (dist >= 0) & (dist <= QBLK)
    dist = dist.astype(F32)

    groups = ((q0, k0, v0, o0, l0), (q1, k1, v1, o1, l1), (q2, k2, v2, o2, l2))
    for g, (_, dil) in enumerate(DSWA_PATTERNS):
        q_ref, k_ref, v_ref, o_scr, l_scr = groups[g]
        n_blocks = SEQ // (dil * QBLK)

        def rows(start, dil=dil):
            if dil == 1:
                return pl.ds(pl.multiple_of(start, QBLK), QBLK)
            return pl.ds(start, QBLK, stride=dil)

        def natural_rows(idx, rows=rows, n_blocks=n_blocks, dil=dil):
            return rows(dil * QBLK * (idx % n_blocks) + idx // n_blocks)

        def reorder(idx, gather, carry=0):
            dst = pl.ds(pl.multiple_of(idx * QBLK, QBLK), QBLK)
            qp[dst, :] = (gather(q_ref) * (HEAD_DIM ** -0.5)).astype(BF16)
            kt[idx] = gather(k_ref).T.astype(BF16)
            v = gather(v_ref)
            va[dst, :] = jnp.where(head0, v, 1.0).astype(BF16)
            vb[dst, :] = jnp.where(head0, 1.0, v).astype(BF16)
            return carry

        if dil == 1:
            def reorder_natural(idx, carry, reorder=reorder):
                src = pl.ds(pl.multiple_of(idx * QBLK, QBLK), QBLK)
                return reorder(idx, lambda ref: ref[src, :].astype(F32), carry)

            lax.fori_loop(0, n_qblk, reorder_natural, 0, unroll=4)
        else:
            for idx in range(n_qblk):
                first_token = dil * QBLK * (idx % n_blocks) + idx // n_blocks

                def gather_strided(ref, first_token=first_token, dil=dil):
                    words = ref.bitcast(jnp.uint32)[pl.ds(first_token // 2, QBLK, stride=dil // 2), :]
                    bits = (words << 16) if first_token % 2 == 0 else (words & jnp.uint32(0xFFFF0000))
                    return pltpu.bitcast(bits, F32)

                reorder(idx, gather_strided)

        bias = []
        for hh in range(2):
            slope = slopes_ref[g * HEADS_PER_PATTERN + 2 * pair + hh] * float(dil)
            bias.append(jnp.where(valid, -slope * dist, NEG_INF))

        def block(idx, with_prev, o_scr=o_scr, l_scr=l_scr, bias=bias, natural_rows=natural_rows):
            cur = pl.ds(pl.multiple_of(idx * QBLK, QBLK), QBLK)
            q = qp[cur, :]
            if with_prev:
                keys_t = jnp.concatenate([kt[idx - 1], kt[idx]], axis=1)
                kv_rows = pl.ds(pl.multiple_of((idx - 1) * QBLK, QBLK), 2 * QBLK)
            else:
                keys_t = kt[idx]
                kv_rows = cur
            pvs, ms = [], []
            for hh, v_scr in ((0, va), (1, vb)):
                qh = jnp.where(head0 == (hh == 0), q, jnp.zeros_like(q))
                s = jnp.dot(qh, keys_t, preferred_element_type=F32)
                s = s + (bias[hh] if with_prev else bias[hh][:, QBLK:])
                m = jnp.max(s, axis=-1, keepdims=True)
                p = jnp.exp(s - m).astype(BF16)
                pvs.append(jnp.dot(p, v_scr[kv_rows, :], preferred_element_type=F32))
                ms.append(m)
            num = jnp.where(head0, pvs[0], pvs[1])
            den = pltpu.roll(jnp.where(head0, pvs[1], pvs[0]), HEAD_DIM, axis=1)
            dst = natural_rows(idx)
            o_scr[dst, :] = num / den
            l_scr[dst, :] = jnp.where(head0, ms[0], ms[1]) + jnp.log(den)

        def first_blocks(res, carry, block=block, n_blocks=n_blocks):
            block(res * n_blocks, False)
            return carry

        lax.fori_loop(0, dil, first_blocks, 0, unroll=min(dil, ATTN_UNROLL_FIRST))

        if n_blocks > 1:
            later = n_blocks - 1

            def later_blocks(k, carry, block=block, later=later, n_blocks=n_blocks):
                block((k // later) * n_blocks + k % later + 1, True)
                return carry

            lax.fori_loop(0, dil * later, later_blocks, 0, unroll=ATTN_UNROLL_LATER[dil])

    def merge(k, carry):
        r = pl.ds(pl.multiple_of(k * QBLK, QBLK), QBLK)
        la, lb, lc = l0[r, :], l1[r, :], l2[r, :]
        m = jnp.maximum(jnp.maximum(la, lb), lc)
        wa, wb, wc = jnp.exp(la - m), jnp.exp(lb - m), jnp.exp(lc - m)
        tot = wa + wb + wc
        o_ref[r, :] = ((wa * o0[r, :] + wb * o1[r, :] + wc * o2[r, :]) / tot).astype(BF16)
        return carry

    lax.fori_loop(0, n_qblk, merge, 0, unroll=2)


def _attention(proj, slopes):
    pairs = HEADS_PER_PATTERN // 2

    def col(base, g):
        return lambda b, p: (b, (base + g * HEADS_PER_PATTERN * HEAD_DIM) // LANES + p)

    in_specs = [pl.BlockSpec(memory_space=pltpu.SMEM)]
    for g in range(len(DSWA_PATTERNS)):
        for base in (OFF_Q, OFF_K, OFF_V):
            in_specs.append(pl.BlockSpec((SEQ, LANES), col(base, g)))
    scr = pltpu.VMEM((SEQ, LANES), F32)
    seq_b = pltpu.VMEM((SEQ, LANES), BF16)
    kt_b = pltpu.VMEM((SEQ // QBLK, LANES, QBLK), BF16)
    return pl.pallas_call(
        _attn_kernel,
        out_shape=jax.ShapeDtypeStruct((TOKENS, ATTN_WIDTH), BF16),
        grid=(BATCH, pairs),
        in_specs=in_specs,
        out_specs=pl.BlockSpec((SEQ, LANES), lambda b, p: (b, p)),
        scratch_shapes=[seq_b, kt_b, seq_b, seq_b] + [scr] * 6,
        compiler_params=_cparams(("arbitrary", "arbitrary")),
        name="dilated_attention",
    )(slopes, *([proj] * 9))


N_MERGE_CHUNKS = D_MODEL // TN_MERGE


def _merge_kernel(*refs):
    (s_ref, a_ref, cb_ref, cc_ref, ch_ref, cch_ref, chh_ref), refs = refs[:7], refs[7:]
    gate_refs, refs = refs[:N_BRANCH * N_MERGE_CHUNKS], refs[N_BRANCH * N_MERGE_CHUNKS:]
    (x_ref, gt_ref, sh2_ref, sc2_ref, gpost_ref, gffn_ref, cw_ref, wss_ref, wat_ref, wcv_ref, wo_ref,
     o_ref, h2_ref, z_scr, m_scr) = refs
    i = pl.program_id(0)
    tiles_per_seq = SEQ // TM_MERGE
    b = i // tiles_per_seq
    seq_start = (i % tiles_per_seq) == 0

    halo = cch_ref[...].astype(F32) * chh_ref[...].astype(F32)
    z_scr[0:CONV_HALO, :] = jnp.where(seq_start, 0.0, halo)
    z_scr[CONV_HALO:CONV_HALO + TM_MERGE, :] = cc_ref[...].astype(F32) * ch_ref[...].astype(F32)
    conv = (cw_ref[0:1, :] * z_scr[pl.ds(CONV_HALO, TM_MERGE), :]
            + cw_ref[1:2, :] * z_scr[pl.ds(CONV_HALO - 1, TM_MERGE), :]
            + cw_ref[2:3, :] * z_scr[pl.ds(CONV_HALO - 2, TM_MERGE), :])
    cv = (cb_ref[...].astype(F32) * conv).astype(BF16)

    s = s_ref[...]
    a = a_ref[...]
    for c in range(N_MERGE_CHUNKS):
        cols = slice(c * TN_MERGE, (c + 1) * TN_MERGE)
        y_ssm = jnp.dot(s, wss_ref[:, cols], preferred_element_type=F32)
        y_att = jnp.dot(a, wat_ref[:, cols], preferred_element_type=F32)
        y_cv = jnp.dot(cv, wcv_ref[:, cols], preferred_element_type=F32)
        merged = (gate_refs[c][...].astype(F32) * y_ssm
                  + gate_refs[N_MERGE_CHUNKS + c][...].astype(F32) * y_att
                  + gate_refs[2 * N_MERGE_CHUNKS + c][...].astype(F32) * y_cv)
        m_scr[:, cols] = merged.astype(BF16)

    y = jnp.dot(m_scr[...], wo_ref[...], preferred_element_type=F32)
    ms = jnp.mean(y * y, axis=-1, keepdims=True)
    yn = y * lax.rsqrt(ms + RMS_EPS) * gpost_ref[...]
    x_new = x_ref[...] + gt_ref[pl.ds(b, 1), :] * yn
    o_ref[...] = x_new
    h2_ref[...] = _norm_mod(x_new, gffn_ref[...], sc2_ref[pl.ds(b, 1), :], sh2_ref[pl.ds(b, 1), :])


def _merge(l, s, a, proj, x, mod, g_post3, g_ffn3, conv_w, w_ssm_out, w_attn_out, w_conv_out, w_o):
    tm = TM_MERGE
    d = D_MODEL
    cw = CONV_WIDTH
    cb_blk = OFF_CONV // cw
    halo_blocks = tm // CONV_HALO

    def halo_map(col_blk):
        return lambda i: (jnp.maximum(i * halo_blocks - 1, 0), col_blk)

    def gate_spec(branch, chunk):
        blk = (OFF_GATE + branch * d) // TN_MERGE + chunk
        return pl.BlockSpec((tm, TN_MERGE), lambda i: (i, blk))

    gate_specs = [gate_spec(br, c) for br in range(N_BRANCH) for c in range(N_MERGE_CHUNKS)]
    tile = pl.BlockSpec((tm, d), lambda i: (i, 0))
    return pl.pallas_call(
        _merge_kernel,
        out_shape=(jax.ShapeDtypeStruct((TOKENS, d), F32), jax.ShapeDtypeStruct((TOKENS, d), BF16)),
        grid=(TOKENS // tm,),
        in_specs=[
            pl.BlockSpec((tm, SSM_WIDTH), lambda i: (i, 0)),
            pl.BlockSpec((tm, ATTN_WIDTH), lambda i: (i, 0)),
            pl.BlockSpec((tm, cw), lambda i: (i, cb_blk)),
            pl.BlockSpec((tm, cw), lambda i: (i, cb_blk + 1)),
            pl.BlockSpec((tm, cw), lambda i: (i, cb_blk + 2)),
            pl.BlockSpec((CONV_HALO, cw), halo_map(cb_blk + 1)),
            pl.BlockSpec((CONV_HALO, cw), halo_map(cb_blk + 2)),
        ] + gate_specs + [
            tile,
            pl.BlockSpec((None, BATCH, d), lambda i: (l, 0, 2)),
            pl.BlockSpec((None, BATCH, d), lambda i: (l, 0, 3)),
            pl.BlockSpec((None, BATCH, d), lambda i: (l, 0, 4)),
            _resident((None, 1, d), lambda i: (l, 0, 0)),
            _resident((None, 1, d), lambda i: (l, 0, 0)),
            _resident((None, 3, cw), lambda i: (l, 0, 0)),
            _resident((None, SSM_WIDTH, d), lambda i: (l, 0, 0)),
            _resident((None, ATTN_WIDTH, d), lambda i: (l, 0, 0)),
            _resident((None, cw, d), lambda i: (l, 0, 0)),
            _resident((None, d, d), lambda i: (l, 0, 0)),
        ],
        out_specs=(tile, tile),
        scratch_shapes=[pltpu.VMEM((tm + CONV_HALO, cw), F32), pltpu.VMEM((tm, d), BF16)],
        compiler_params=_cparams(("arbitrary",)),
        name="branch_merge",
    )(s, a, *([proj] * (5 + len(gate_specs))), x, mod, mod, mod, g_post3, g_ffn3, conv_w,
      w_ssm_out, w_attn_out, w_conv_out, w_o)


def _up_kernel(h_ref, wa_ref, wb_ref, cwa_ref, cwb_ref, o_ref):
    half = SEQ // 2
    row = lax.broadcasted_iota(jnp.int32, (half, 1), 0)
    for first in (True, False):
        lead = 0 if first else UP_HALO
        h = h_ref[pl.ds(0 if first else half - UP_HALO, half + lead), :]
        convs = []
        for w_ref, cw_ref in ((wa_ref, cwa_ref), (wb_ref, cwb_ref)):
            z = jnp.dot(h, w_ref[...], preferred_element_type=F32)
            z1 = pltpu.roll(z, 1, axis=0)
            z2 = pltpu.roll(z, 2, axis=0)
            if first:
                z1 = jnp.where(row >= 1, z1, 0.0)
                z2 = jnp.where(row >= 2, z2, 0.0)
            y = cw_ref[0:1, :] * z + cw_ref[1:2, :] * z1 + cw_ref[2:3, :] * z2
            convs.append(y[lead:, :])
        ca, cb = convs
        o_ref[pl.ds(0 if first else half, half), :] = (ca * _sigmoid(ca) * cb).astype(BF16)


def _mlp_up(l, h, w_up, conv_w):
    d = D_MODEL
    tf = TF_UP
    nf = D_FF // tf
    return pl.pallas_call(
        _up_kernel,
        out_shape=jax.ShapeDtypeStruct((TOKENS, D_FF), BF16),
        grid=(BATCH, nf),
        in_specs=[
            pl.BlockSpec((SEQ, d), lambda i, j: (i, 0)),
            pl.BlockSpec((None, d, tf), lambda i, j: (l, 0, j)),
            pl.BlockSpec((None, d, tf), lambda i, j: (l, 0, nf + j)),
            pl.BlockSpec((None, 3, tf), lambda i, j: (l, 0, j)),
            pl.BlockSpec((None, 3, tf), lambda i, j: (l, 0, nf + j)),
        ],
        out_specs=pl.BlockSpec((SEQ, tf), lambda i, j: (i, j)),
        compiler_params=_cparams(("arbitrary", "arbitrary")),
        name="mlp_up",
    )(h, w_up, w_up, conv_w, conv_w)


def _down_kernel(act_ref, w_ref, x_ref, gt_ref, gpost_ref, *rest):
    b = pl.program_id(0) // (SEQ // TM_DOWN)
    y = jnp.dot(act_ref[...], w_ref[...], preferred_element_type=F32)
    ms = jnp.mean(y * y, axis=-1, keepdims=True)
    yn = y * lax.rsqrt(ms + RMS_EPS) * gpost_ref[...]
    x_new = x_ref[...] + gt_ref[pl.ds(b, 1), :] * yn
    if len(rest) == 1:
        (o_ref,) = rest
        o_ref[...] = x_new
    else:
        sh_ref, sc_ref, gnext_ref, o_ref, hn_ref = rest
        o_ref[...] = x_new
        hn_ref[...] = _norm_mod(x_new, gnext_ref[...], sc_ref[pl.ds(b, 1), :], sh_ref[pl.ds(b, 1), :])


def _mlp_down(l, act, w_down, x, mod, g_post3, g_mix3):
    d = D_MODEL
    tm = TM_DOWN
    last = l == DEPTH - 1
    tile = pl.BlockSpec((tm, d), lambda i: (i, 0))
    in_specs = [
        pl.BlockSpec((tm, D_FF), lambda i: (i, 0)),
        _resident((None, D_FF, d), lambda i: (l, 0, 0)),
        tile,
        pl.BlockSpec((None, BATCH, d), lambda i: (l, 0, 5)),
        _resident((None, 1, d), lambda i: (l, 0, 0)),
    ]
    args = [act, w_down, x, mod, g_post3]
    x_shape = jax.ShapeDtypeStruct((TOKENS, d), F32)
    if last:
        out_shape, out_specs = x_shape, tile
    else:
        in_specs += [
            pl.BlockSpec((None, BATCH, d), lambda i: (l + 1, 0, 0)),
            pl.BlockSpec((None, BATCH, d), lambda i: (l + 1, 0, 1)),
            _resident((None, 1, d), lambda i: (l + 1, 0, 0)),
        ]
        args += [mod, mod, g_mix3]
        out_shape, out_specs = (x_shape, jax.ShapeDtypeStruct((TOKENS, d), BF16)), (tile, tile)
    out = pl.pallas_call(
        _down_kernel,
        out_shape=out_shape,
        grid=(TOKENS // tm,),
        in_specs=in_specs,
        out_specs=out_specs,
        compiler_params=_cparams(("arbitrary",)),
        name="mlp_down",
    )(*args)
    return (out, None) if last else out


def _alibi_slopes():
    h = N_ATTN_HEADS
    return np.array([2.0 ** (-8.0 * (i + 1) / h) for i in range(h)], dtype=np.float32)


def _time_major_permutation():
    rows = BATCH * SSM_CHUNK
    i = np.arange(rows)
    src = (i % BATCH) * SSM_CHUNK + i // BATCH
    p = np.zeros((rows, rows), np.float32)
    p[i, src] = 1.0
    return p


def kernel(x, c, w_mod, b_mod, g_pre_mix, g_post_mix, g_pre_ffn, g_post_ffn, w_in, ssm_log_dt, ssm_a_re,
           ssm_a_im, ssm_b_re, ssm_b_im, ssm_c_re, ssm_c_im, ssm_d, w_glu, b_glu, conv_mix_w, w_ssm_out,
           w_attn_out, w_conv_out, b_gate, w_o, w_up, ffn_conv_w, w_down):
    w_in_b = w_in.astype(BF16)
    bias3 = _rows3(jnp.concatenate([jnp.zeros((DEPTH, OFF_GATE), F32), b_gate], axis=-1))
    w_glu_b = w_glu.astype(BF16)
    w_ssm_out_b = w_ssm_out.astype(BF16)
    w_attn_out_b = w_attn_out.astype(BF16)
    w_conv_out_b = w_conv_out.astype(BF16)
    w_o_b = w_o.astype(BF16)
    w_down_b = w_down.astype(BF16)
    w_up_b = w_up.astype(BF16)
    g_pre_mix3, g_post_mix3 = _rows3(g_pre_mix), _rows3(g_post_mix)
    g_pre_ffn3, g_post_ffn3 = _rows3(g_pre_ffn), _rows3(g_post_ffn)
    d_skip3, b_glu3 = _rows3(ssm_d), _rows3(b_glu)

    mod = _modulation(c, w_mod, b_mod)
    lr, li, bbr, bbi = _ssm_params(ssm_log_dt, ssm_a_re, ssm_a_im, ssm_b_re, ssm_b_im)
    perm_np = _time_major_permutation()
    perm = jnp.asarray(perm_np, dtype=BF16)
    permt = jnp.asarray(perm_np.T, dtype=BF16)
    slopes = jnp.asarray(_alibi_slopes())

    xf = x.reshape(TOKENS, D_MODEL)
    h = _first_norm(xf, g_pre_mix3, mod)
    for l in range(DEPTH):
        proj = _in_projection(l, h, w_in_b, bias3)
        bb, cre, cim, lam = _ssm_layout(lr[l], li[l], bbr[:, l], bbi[:, l], ssm_c_re[l], ssm_c_im[l])
        s = _ssm_mixer(l, proj.reshape(BATCH, SEQ, N_IN), perm, permt, bb, cre, cim, lam,
                       d_skip3, w_glu_b, b_glu3)
        a = _attention(proj, slopes)
        xf, h = _merge(l, s.reshape(TOKENS, SSM_WIDTH), a, proj, xf, mod, g_post_mix3, g_pre_ffn3, conv_mix_w,
                       w_ssm_out_b, w_attn_out_b, w_conv_out_b, w_o_b)
        act = _mlp_up(l, h, w_up_b, ffn_conv_w)
        xf, h = _mlp_down(l, act, w_down_b, xf, mod, g_post_ffn3, g_pre_mix3)
    return xf.reshape(BATCH, SEQ, D_MODEL)
```

```python
import numpy as np
import jax
import jax.numpy as jnp
from jax import lax
from jax.experimental import pallas as pl
from jax.experimental.pallas import tpu as pltpu

F32 = jnp.float32
BF16 = jnp.bfloat16

D_MODEL = 2048
BATCH = 8
SEQ = 2048
DEPTH = 4
TOKENS = BATCH * SEQ
RMS_EPS = 1e-6
NEG_INF = -1e30
N_BRANCH = 3
SSM_WIDTH = 512
SSM_GROUP = 16
SSM_GROUPS = 32
SSM_STATE = 64
HEAD_DIM = 64
DSWA_PATTERNS = ((128, 1), (512, 4), (2048, 16))
ATTN_WIDTH = 512
HEADS_PER_PATTERN = 8
N_ATTN_HEADS = 24
QKV_WIDTH = 1536
CONV_WIDTH = 512
D_FF = 5632
OFF_U = 0
OFF_Q = SSM_WIDTH
OFF_K = OFF_Q + QKV_WIDTH
OFF_V = OFF_K + QKV_WIDTH
OFF_CONV = OFF_V + QKV_WIDTH
OFF_GATE = OFF_CONV + 3 * CONV_WIDTH
N_IN = OFF_GATE + N_BRANCH * D_MODEL

LANES = 128
VMEM_LIMIT = 60 * 1024 * 1024

TM_NORM = 1024
TN_PROJ = 1280
TF_UP = 512
UP_HALO = 16
TM_MERGE = 256
CONV_HALO = 16
TN_MERGE = 512
TM_DOWN = 256
SSM_CHUNK = 128
QBLK = 128


def _cparams(sem):
    return pltpu.CompilerParams(dimension_semantics=sem, vmem_limit_bytes=VMEM_LIMIT)


def _resident(shape, index_map):
    return pl.BlockSpec(shape, index_map, pipeline_mode=pl.Buffered(1))


def _sigmoid(x):
    return 0.5 * jnp.tanh(0.5 * x) + 0.5


def _rows3(a):
    return a.reshape(DEPTH, 1, a.shape[-1])


def _norm_mod(x, g, scale_row, shift_row):
    ms = jnp.mean(x * x, axis=-1, keepdims=True)
    y = x * lax.rsqrt(ms + RMS_EPS) * g
    return (y * (1.0 + scale_row) + shift_row).astype(BF16)


def _mod_kernel(c_ref, w_ref, b_ref, o_ref):
    c = c_ref[...]
    cond = (c * jax.nn.sigmoid(c)).astype(BF16)
    o_ref[0] = jnp.dot(cond, w_ref[0].astype(BF16), preferred_element_type=F32) + b_ref[0]


def _modulation(c, w_mod, b_mod):
    tn = 1024
    n = 6 * D_MODEL
    return pl.pallas_call(
        _mod_kernel,
        out_shape=jax.ShapeDtypeStruct((DEPTH, BATCH, n), F32),
        grid=(DEPTH, n // tn),
        in_specs=[
            pl.BlockSpec((BATCH, D_MODEL), lambda l, j: (0, 0)),
            pl.BlockSpec((1, D_MODEL, tn), lambda l, j: (l, 0, j)),
            pl.BlockSpec((1, 1, tn), lambda l, j: (l, 0, j)),
        ],
        out_specs=pl.BlockSpec((1, BATCH, tn), lambda l, j: (l, 0, j)),
        compiler_params=_cparams(("arbitrary", "arbitrary")),
        name="modulation",
    )(c, w_mod, b_mod.reshape(DEPTH, 1, n))


def _norm_kernel(x_ref, g_ref, sh_ref, sc_ref, o_ref):
    b = pl.program_id(0) // (SEQ // TM_NORM)
    rows = 128

    def body(k, carry):
        r = pl.ds(pl.multiple_of(k * rows, rows), rows)
        o_ref[r, :] = _norm_mod(x_ref[r, :], g_ref[...], sc_ref[pl.ds(b, 1), :], sh_ref[pl.ds(b, 1), :])
        return carry

    lax.fori_loop(0, TM_NORM // rows, body, 0)


def _first_norm(x, g3, mod):
    d = D_MODEL
    return pl.pallas_call(
        _norm_kernel,
        out_shape=jax.ShapeDtypeStruct((TOKENS, d), BF16),
        grid=(TOKENS // TM_NORM,),
        in_specs=[
            pl.BlockSpec((TM_NORM, d), lambda i: (i, 0)),
            pl.BlockSpec((None, 1, d), lambda i: (0, 0, 0)),
            pl.BlockSpec((None, BATCH, d), lambda i: (0, 0, 0)),
            pl.BlockSpec((None, BATCH, d), lambda i: (0, 0, 1)),
        ],
        out_specs=pl.BlockSpec((TM_NORM, d), lambda i: (i, 0)),
        compiler_params=_cparams(("arbitrary",)),
        name="first_norm",
    )(x, g3, mod, mod)


def _inproj_kernel(h_ref, w_ref, bias_ref, o_ref):
    j = pl.program_id(1)
    raw_tiles = OFF_GATE // TN_PROJ
    raw_cols_in_mixed = OFF_GATE - raw_tiles * TN_PROJ

    @pl.when(j < raw_tiles)
    def _():
        o_ref[...] = jnp.dot(h_ref[...], w_ref[...].astype(BF16), preferred_element_type=F32).astype(BF16)

    @pl.when(j == raw_tiles)
    def _():
        acc = jnp.dot(h_ref[...], w_ref[...].astype(BF16), preferred_element_type=F32)
        col = lax.broadcasted_iota(jnp.int32, acc.shape, 1)
        o_ref[...] = jnp.where(col < raw_cols_in_mixed, acc, _sigmoid(acc + bias_ref[...])).astype(BF16)

    @pl.when(j > raw_tiles)
    def _():
        acc = jnp.dot(h_ref[...], w_ref[...].astype(BF16), preferred_element_type=F32)
        o_ref[...] = _sigmoid(acc + bias_ref[...]).astype(BF16)


def _in_projection(l, h, w_in, bias3):
    d = D_MODEL
    return pl.pallas_call(
        _inproj_kernel,
        out_shape=jax.ShapeDtypeStruct((TOKENS, N_IN), BF16),
        grid=(BATCH, N_IN // TN_PROJ),
        in_specs=[
            pl.BlockSpec((SEQ, d), lambda i, j: (i, 0)),
            pl.BlockSpec((None, d, TN_PROJ), lambda i, j: (l, 0, j)),
            pl.BlockSpec((None, 1, TN_PROJ), lambda i, j: (l, 0, j)),
        ],
        out_specs=pl.BlockSpec((SEQ, TN_PROJ), lambda i, j: (i, j)),
        compiler_params=_cparams(("arbitrary", "arbitrary")),
        name="in_projection",
    )(h, w_in, bias3)


def _ssm_param_kernel(ldt_ref, ar_ref, ai_ref, br_ref, bi_ref, lr_ref, li_ref, bbr_ref, bbi_ref):
    dt = jnp.exp(ldt_ref[...])
    ar = ar_ref[...]
    ai = ai_ref[...]
    mag = jnp.exp(ar * dt)
    lr = mag * jnp.cos(ai * dt)
    li = mag * jnp.sin(ai * dt)
    den = ar * ar + ai * ai
    fr = ((lr - 1.0) * ar + li * ai) / den
    fi = (li * ar - (lr - 1.0) * ai) / den
    lr_ref[...] = lr
    li_ref[...] = li
    for ch in range(SSM_GROUP):
        br = br_ref[ch]
        bi = bi_ref[ch]
        bbr_ref[ch] = fr * br - fi * bi
        bbi_ref[ch] = fr * bi + fi * br


def _ssm_params(log_dt, a_re, a_im, b_re, b_im):
    n = DEPTH * SSM_GROUPS
    mat = jax.ShapeDtypeStruct((n, SSM_STATE), F32)
    cube = jax.ShapeDtypeStruct((SSM_GROUP, n, SSM_STATE), F32)

    def channel_major(b):
        return b.transpose(3, 0, 1, 2).reshape(SSM_GROUP, n, SSM_STATE)

    lr, li, bbr, bbi = pl.pallas_call(
        _ssm_param_kernel,
        out_shape=(mat, mat, cube, cube),
        name="ssm_params",
    )(log_dt.reshape(n, 1), a_re.reshape(n, SSM_STATE), a_im.reshape(n, SSM_STATE),
      channel_major(b_re), channel_major(b_im))
    shape3 = (DEPTH, SSM_GROUPS, SSM_STATE)
    return (lr.reshape(shape3), li.reshape(shape3),
            bbr.reshape((SSM_GROUP,) + shape3), bbi.reshape((SSM_GROUP,) + shape3))


N_SLAB = SSM_WIDTH // LANES
GROUPS_PER_SLAB = LANES // SSM_GROUP
STATES_PER_SLAB = GROUPS_PER_SLAB * SSM_STATE
N_STATES = SSM_GROUPS * SSM_STATE


def _ssm_layout(lr, li, bbr, bbi, c_re, c_im):
    eye = jnp.eye(GROUPS_PER_SLAB, dtype=F32)

    def in_blocks(bb):
        t = bb.reshape(SSM_GROUP, N_SLAB, GROUPS_PER_SLAB, SSM_STATE).transpose(1, 2, 0, 3)
        blk = eye[None, :, None, :, None] * t[:, :, :, None, :]
        return blk.reshape(N_SLAB, LANES, STATES_PER_SLAB)

    def out_blocks(cc):
        t = cc.reshape(N_SLAB, GROUPS_PER_SLAB, SSM_GROUP, SSM_STATE).transpose(0, 1, 3, 2)
        blk = eye[None, :, None, :, None] * t[:, :, :, None, :]
        return blk.reshape(N_SLAB, STATES_PER_SLAB, LANES)

    bb = jnp.concatenate([in_blocks(bbr), in_blocks(bbi)], axis=-1).astype(BF16)
    lam = jnp.stack([lr.reshape(N_STATES), li.reshape(N_STATES)], axis=0)
    return bb, out_blocks(c_re).astype(BF16), out_blocks(c_im).astype(BF16), lam


def _ssm_kernel(u_ref, bb_ref, cre_ref, cim_ref, lam_ref, dsk_ref, wglu_ref, bglu_ref,
                o_ref, x_scr, h_scr, *stage):
    u_stage, o_stage = stage[:N_SLAB], stage[N_SLAB:]

    @pl.when(pl.program_id(0) == 0)
    def _():
        h_scr[...] = jnp.zeros_like(h_scr)

    for b in range(BATCH):
        u_b = u_ref[b].astype(F32)
        for s in range(N_SLAB):
            u_stage[s][pl.ds(b, SSM_CHUNK, stride=BATCH), :] = u_b[:, s * LANES:(s + 1) * LANES]
    u_tm = jnp.concatenate([u_stage[s][...] for s in range(N_SLAB)], axis=-1)
    u_tm_b = u_tm.astype(BF16)

    for s in range(N_SLAB):
        xs = jnp.dot(u_tm_b[:, s * LANES:(s + 1) * LANES], bb_ref[s], preferred_element_type=F32)
        x_scr[:, s * STATES_PER_SLAB:(s + 1) * STATES_PER_SLAB] = xs[:, :STATES_PER_SLAB]
        x_scr[:, N_STATES + s * STATES_PER_SLAB:N_STATES + (s + 1) * STATES_PER_SLAB] = xs[:, STATES_PER_SLAB:]

    width = STATES_PER_SLAB
    for s in range(N_STATES // width):
        re_cols = pl.ds(s * width, width)
        im_cols = pl.ds(N_STATES + s * width, width)
        lr = jnp.broadcast_to(lam_ref[0:1, s * width:(s + 1) * width], (BATCH, width))
        li = jnp.broadcast_to(lam_ref[1:2, s * width:(s + 1) * width], (BATCH, width))

        def step(t, carry, re_cols=re_cols, im_cols=im_cols, lr=lr, li=li):
            hr, hi = carry
            r0 = pl.multiple_of(t * BATCH, BATCH)
            xr = x_scr[pl.ds(r0, BATCH), re_cols]
            xi = x_scr[pl.ds(r0, BATCH), im_cols]
            nr = lr * hr - li * hi + xr
            ni = lr * hi + li * hr + xi
            x_scr[pl.ds(r0, BATCH), re_cols] = nr
            x_scr[pl.ds(r0, BATCH), im_cols] = ni
            return nr, ni

        hr, hi = lax.fori_loop(0, SSM_CHUNK, step, (h_scr[:, re_cols], h_scr[:, im_cols]), unroll=4)
        h_scr[:, re_cols] = hr
        h_scr[:, im_cols] = hi

    ys = []
    for s in range(N_SLAB):
        h_re = x_scr[:, s * STATES_PER_SLAB:(s + 1) * STATES_PER_SLAB].astype(BF16)
        h_im = x_scr[:, N_STATES + s * STATES_PER_SLAB:N_STATES + (s + 1) * STATES_PER_SLAB].astype(BF16)
        ys.append(jnp.dot(h_re, cre_ref[s], preferred_element_type=F32)
                  - jnp.dot(h_im, cim_ref[s], preferred_element_type=F32))
    y = jnp.concatenate(ys, axis=-1) + dsk_ref[...] * u_tm
    g = jax.nn.gelu(y)
    gate = jnp.dot(g.astype(BF16), wglu_ref[...], preferred_element_type=F32) + bglu_ref[...]
    out_tm = g * _sigmoid(gate)
    for s in range(N_SLAB):
        o_stage[s][...] = out_tm[:, s * LANES:(s + 1) * LANES]
    for b in range(BATCH):
        rows_b = [o_stage[s][pl.ds(b, SSM_CHUNK, stride=BATCH), :] for s in range(N_SLAB)]
        o_ref[b] = jnp.concatenate(rows_b, axis=-1).astype(BF16)


def _ssm_mixer(l, proj3, bb, cre, cim, lam, d_skip3, w_glu, b_glu3):
    rows = BATCH * SSM_CHUNK
    return pl.pallas_call(
        _ssm_kernel,
        out_shape=jax.ShapeDtypeStruct((BATCH, SEQ, SSM_WIDTH), BF16),
        grid=(SEQ // SSM_CHUNK,),
        in_specs=[
            pl.BlockSpec((BATCH, SSM_CHUNK, SSM_WIDTH), lambda c: (0, c, OFF_U // SSM_WIDTH)),
            _resident((N_SLAB, LANES, 2 * STATES_PER_SLAB), lambda c: (0, 0, 0)),
            _resident((N_SLAB, STATES_PER_SLAB, LANES), lambda c: (0, 0, 0)),
            _resident((N_SLAB, STATES_PER_SLAB, LANES), lambda c: (0, 0, 0)),
            _resident((2, N_STATES), lambda c: (0, 0)),
            _resident((None, 1, SSM_WIDTH), lambda c: (l, 0, 0)),
            _resident((None, SSM_WIDTH, SSM_WIDTH), lambda c: (l, 0, 0)),
            _resident((None, 1, SSM_WIDTH), lambda c: (l, 0, 0)),
        ],
        out_specs=pl.BlockSpec((BATCH, SSM_CHUNK, SSM_WIDTH), lambda c: (0, c, 0)),
        scratch_shapes=[pltpu.VMEM((rows, 2 * N_STATES), F32), pltpu.VMEM((BATCH, 2 * N_STATES), F32)]
        + [pltpu.VMEM((rows, LANES), F32)] * (2 * N_SLAB),
        compiler_params=_cparams(("arbitrary",)),
        name="ssm_mixer",
    )(proj3, bb, cre, cim, lam, d_skip3, w_glu, b_glu3)


ATTN_UNROLL_LATER = {1: 5, 4: 6}
ATTN_UNROLL_FIRST = 8


def _attn_kernel(slopes_ref, q0, k0, v0, q1, k1, v1, q2, k2, v2, o_ref,
                 qp, kt, va, vb, o0, l0, o1, l1, o2, l2):
    pair = pl.program_id(1)
    n_qblk = SEQ // QBLK
    lane = lax.broadcasted_iota(jnp.int32, (QBLK, LANES), 1)
    head0 = lane < HEAD_DIM
    qi = lax.broadcasted_iota(jnp.int32, (QBLK, 2 * QBLK), 0)
    kj = lax.broadcasted_iota(jnp.int32, (QBLK, 2 * QBLK), 1)
    dist = QBLK + qi - kj
    valid = (dist >= 0) & (dist <= QBLK)
    dist = dist.astype(F32)

    groups = ((q0, k0, v0, o0, l0), (q1, k1, v1, o1, l1), (q2, k2, v2, o2, l2))
    for g, (_, dil) in enumerate(DSWA_PATTERNS):
        q_ref, k_ref, v_ref, o_scr, l_scr = groups[g]
        n_blocks = SEQ // (dil * QBLK)

        def rows(start, dil=dil):
            if dil == 1:
                return pl.ds(pl.multiple_of(start, QBLK), QBLK)
            return pl.ds(start, QBLK, stride=dil)

        def natural_rows(idx, rows=rows, n_blocks=n_blocks, dil=dil):
            return rows(dil * QBLK * (idx % n_blocks) + idx // n_blocks)

        def reorder(idx, gather, carry=0):
            dst = pl.ds(pl.multiple_of(idx * QBLK, QBLK), QBLK)
            qp[dst, :] = (gather(q_ref) * (HEAD_DIM ** -0.5)).astype(BF16)
            kt[idx] = gather(k_ref).T.astype(BF16)
            v = gather(v_ref)
            va[dst, :] = jnp.where(head0, v, 1.0).astype(BF16)
            vb[dst, :] = jnp.where(head0, 1.0, v).astype(BF16)
            return carry

        if dil == 1:
            def reorder_natural(idx, carry, reorder=reorder):
                src = pl.ds(pl.multiple_of(idx * QBLK, QBLK), QBLK)
                return reorder(idx, lambda ref: ref[src, :].astype(F32), carry)

            lax.fori_loop(0, n_qblk, reorder_natural, 0, unroll=4)
        else:
            for idx in range(n_qblk):
                first_token = dil * QBLK * (idx % n_blocks) + idx // n_blocks

                def gather_strided(ref, first_token=first_token, dil=dil):
                    words = ref.bitcast(jnp.uint32)[pl.ds(first_token // 2, QBLK, stride=dil // 2), :]
                    bits = (words << 16) if first_token % 2 == 0 else (words & jnp.uint32(0xFFFF0000))
                    return pltpu.bitcast(bits, F32)

                reorder(idx, gather_strided)

        bias = []
        for hh in range(2):
            slope = slopes_ref[g * HEADS_PER_PATTERN + 2 * pair + hh] * float(dil)
            bias.append(jnp.where(valid, -slope * dist, NEG_INF))

        def block(idx, with_prev, o_scr=o_scr, l_scr=l_scr, bias=bias, natural_rows=natural_rows):
            cur = pl.ds(pl.multiple_of(idx * QBLK, QBLK), QBLK)
            q = qp[cur, :]
            if with_prev:
                keys_t = jnp.concatenate([kt[idx - 1], kt[idx]], axis=1)
                kv_rows = pl.ds(pl.multiple_of((idx - 1) * QBLK, QBLK), 2 * QBLK)
            else:
                keys_t = kt[idx]
                kv_rows = cur
            pvs, ms = [], []
            for hh, v_scr in ((0, va), (1, vb)):
                qh = jnp.where(head0 == (hh == 0), q, jnp.zeros_like(q))
                s = jnp.dot(qh, keys_t, preferred_element_type=F32)
                s = s + (bias[hh] if with_prev else bias[hh][:, QBLK:])
                m = jnp.max(s, axis=-1, keepdims=True)
                p = jnp.exp(s - m).astype(BF16)
                pvs.append(jnp.dot(p, v_scr[kv_rows, :], preferred_element_type=F32))
                ms.append(m)
            num = jnp.where(head0, pvs[0], pvs[1])
            den = pltpu.roll(jnp.where(head0, pvs[1], pvs[0]), HEAD_DIM, axis=1)
            dst = natural_rows(idx)
            o_scr[dst, :] = num / den
            l_scr[dst, :] = jnp.where(head0, ms[0], ms[1]) + jnp.log(den)

        def first_blocks(res, carry, block=block, n_blocks=n_blocks):
            block(res * n_blocks, False)
            return carry

        lax.fori_loop(0, dil, first_blocks, 0, unroll=min(dil, ATTN_UNROLL_FIRST))

        if n_blocks > 1:
            later = n_blocks - 1

            def later_blocks(k, carry, block=block, later=later, n_blocks=n_blocks):
                block((k // later) * n_blocks + k % later + 1, True)
                return carry

            lax.fori_loop(0, dil * later, later_blocks, 0, unroll=ATTN_UNROLL_LATER[dil])

    def merge(k, carry):
        r = pl.ds(pl.multiple_of(k * QBLK, QBLK), QBLK)
        la, lb, lc = l0[r, :], l1[r, :], l2[r, :]
        m = jnp.maximum(jnp.maximum(la, lb), lc)
        wa, wb, wc = jnp.exp(la - m), jnp.exp(lb - m), jnp.exp(lc - m)
        tot = wa + wb + wc
        o_ref[r, :] = ((wa * o0[r, :] + wb * o1[r, :] + wc * o2[r, :]) / tot).astype(BF16)
        return carry

    lax.fori_loop(0, n_qblk, merge, 0, unroll=2)


def _attention(proj, slopes):
    pairs = HEADS_PER_PATTERN // 2

    def col(base, g):
        return lambda b, p: (b, (base + g * HEADS_PER_PATTERN * HEAD_DIM) // LANES + p)

    in_specs = [pl.BlockSpec(memory_space=pltpu.SMEM)]
    for g in range(len(DSWA_PATTERNS)):
        for base in (OFF_Q, OFF_K, OFF_V):
            in_specs.append(pl.BlockSpec((SEQ, LANES), col(base, g)))
    scr = pltpu.VMEM((SEQ, LANES), F32)
    seq_b = pltpu.VMEM((SEQ, LANES), BF16)
    kt_b = pltpu.VMEM((SEQ // QBLK, LANES, QBLK), BF16)
    return pl.pallas_call(
        _attn_kernel,
        out_shape=jax.ShapeDtypeStruct((TOKENS, ATTN_WIDTH), BF16),
        grid=(BATCH, pairs),
        in_specs=in_specs,
        out_specs=pl.BlockSpec((SEQ, LANES), lambda b, p: (b, p)),
        scratch_shapes=[seq_b, kt_b, seq_b, seq_b] + [scr] * 6,
        compiler_params=_cparams(("arbitrary", "arbitrary")),
        name="dilated_attention",
    )(slopes, *([proj] * 9))


N_MERGE_CHUNKS = D_MODEL // TN_MERGE


def _merge_kernel(*refs):
    (s_ref, a_ref, cb_ref, cc_ref, ch_ref, cch_ref, chh_ref), refs = refs[:7], refs[7:]
    gate_refs, refs = refs[:N_BRANCH * N_MERGE_CHUNKS], refs[N_BRANCH * N_MERGE_CHUNKS:]
    (x_ref, gt_ref, sh2_ref, sc2_ref, gpost_ref, gffn_ref, cw_ref, wss_ref, wat_ref, wcv_ref, wo_ref,
     o_ref, h2_ref, z_scr, m_scr) = refs
    i = pl.program_id(0)
    tiles_per_seq = SEQ // TM_MERGE
    b = i // tiles_per_seq
    seq_start = (i % tiles_per_seq) == 0

    halo = cch_ref[...].astype(F32) * chh_ref[...].astype(F32)
    z_scr[0:CONV_HALO, :] = jnp.where(seq_start, 0.0, halo)
    z_scr[CONV_HALO:CONV_HALO + TM_MERGE, :] = cc_ref[...].astype(F32) * ch_ref[...].astype(F32)
    conv = (cw_ref[0:1, :] * z_scr[pl.ds(CONV_HALO, TM_MERGE), :]
            + cw_ref[1:2, :] * z_scr[pl.ds(CONV_HALO - 1, TM_MERGE), :]
            + cw_ref[2:3, :] * z_scr[pl.ds(CONV_HALO - 2, TM_MERGE), :])
    cv = (cb_ref[...].astype(F32) * conv).astype(BF16)

    s = s_ref[...]
    a = a_ref[...]
    for c in range(N_MERGE_CHUNKS):
        cols = slice(c * TN_MERGE, (c + 1) * TN_MERGE)
        y_ssm = jnp.dot(s, wss_ref[:, cols], preferred_element_type=F32)
        y_att = jnp.dot(a, wat_ref[:, cols], preferred_element_type=F32)
        y_cv = jnp.dot(cv, wcv_ref[:, cols], preferred_element_type=F32)
        merged = (gate_refs[c][...].astype(F32) * y_ssm
                  + gate_refs[N_MERGE_CHUNKS + c][...].astype(F32) * y_att
                  + gate_refs[2 * N_MERGE_CHUNKS + c][...].astype(F32) * y_cv)
        m_scr[:, cols] = merged.astype(BF16)

    y = jnp.dot(m_scr[...], wo_ref[...], preferred_element_type=F32)
    ms = jnp.mean(y * y, axis=-1, keepdims=True)
    yn = y * lax.rsqrt(ms + RMS_EPS) * gpost_ref[...]
    x_new = x_ref[...] + gt_ref[pl.ds(b, 1), :] * yn
    o_ref[...] = x_new
    h2_ref[...] = _norm_mod(x_new, gffn_ref[...], sc2_ref[pl.ds(b, 1), :], sh2_ref[pl.ds(b, 1), :])


def _merge(l, s, a, proj, x, mod, g_post3, g_ffn3, conv_w, w_ssm_out, w_attn_out, w_conv_out, w_o):
    tm = TM_MERGE
    d = D_MODEL
    cw = CONV_WIDTH
    cb_blk = OFF_CONV // cw
    halo_blocks = tm // CONV_HALO

    def halo_map(col_blk):
        return lambda i: (jnp.maximum(i * halo_blocks - 1, 0), col_blk)

    def gate_spec(branch, chunk):
        blk = (OFF_GATE + branch * d) // TN_MERGE + chunk
        return pl.BlockSpec((tm, TN_MERGE), lambda i: (i, blk))

    gate_specs = [gate_spec(br, c) for br in range(N_BRANCH) for c in range(N_MERGE_CHUNKS)]
    tile = pl.BlockSpec((tm, d), lambda i: (i, 0))
    return pl.pallas_call(
        _merge_kernel,
        out_shape=(jax.ShapeDtypeStruct((TOKENS, d), F32), jax.ShapeDtypeStruct((TOKENS, d), BF16)),
        grid=(TOKENS // tm,),
        in_specs=[
            pl.BlockSpec((tm, SSM_WIDTH), lambda i: (i, 0)),
            pl.BlockSpec((tm, ATTN_WIDTH), lambda i: (i, 0)),
            pl.BlockSpec((tm, cw), lambda i: (i, cb_blk)),
            pl.BlockSpec((tm, cw), lambda i: (i, cb_blk + 1)),
            pl.BlockSpec((tm, cw), lambda i: (i, cb_blk + 2)),
            pl.BlockSpec((CONV_HALO, cw), halo_map(cb_blk + 1)),
            pl.BlockSpec((CONV_HALO, cw), halo_map(cb_blk + 2)),
        ] + gate_specs + [
            tile,
            pl.BlockSpec((None, BATCH, d), lambda i: (l, 0, 2)),
            pl.BlockSpec((None, BATCH, d), lambda i: (l, 0, 3)),
            pl.BlockSpec((None, BATCH, d), lambda i: (l, 0, 4)),
            _resident((None, 1, d), lambda i: (l, 0, 0)),
            _resident((None, 1, d), lambda i: (l, 0, 0)),
            _resident((None, 3, cw), lambda i: (l, 0, 0)),
            _resident((None, SSM_WIDTH, d), lambda i: (l, 0, 0)),
            _resident((None, ATTN_WIDTH, d), lambda i: (l, 0, 0)),
            _resident((None, cw, d), lambda i: (l, 0, 0)),
            _resident((None, d, d), lambda i: (l, 0, 0)),
        ],
        out_specs=(tile, tile),
        scratch_shapes=[pltpu.VMEM((tm + CONV_HALO, cw), F32), pltpu.VMEM((tm, d), BF16)],
        compiler_params=_cparams(("arbitrary",)),
        name="branch_merge",
    )(s, a, *([proj] * (5 + len(gate_specs))), x, mod, mod, mod, g_post3, g_ffn3, conv_w,
      w_ssm_out, w_attn_out, w_conv_out, w_o)


def _up_kernel(h_ref, wa_ref, wb_ref, cwa_ref, cwb_ref, o_ref):
    half = SEQ // 2
    row = lax.broadcasted_iota(jnp.int32, (half, 1), 0)
    for first in (True, False):
        lead = 0 if first else UP_HALO
        h = h_ref[pl.ds(0 if first else half - UP_HALO, half + lead), :]
        convs = []
        for w_ref, cw_ref in ((wa_ref, cwa_ref), (wb_ref, cwb_ref)):
            z = jnp.dot(h, w_ref[...], preferred_element_type=F32)
            z1 = pltpu.roll(z, 1, axis=0)
            z2 = pltpu.roll(z, 2, axis=0)
            if first:
                z1 = jnp.where(row >= 1, z1, 0.0)
                z2 = jnp.where(row >= 2, z2, 0.0)
            y = cw_ref[0:1, :] * z + cw_ref[1:2, :] * z1 + cw_ref[2:3, :] * z2
            convs.append(y[lead:, :])
        ca, cb = convs
        o_ref[pl.ds(0 if first else half, half), :] = (ca * _sigmoid(ca) * cb).astype(BF16)


def _mlp_up(l, h, w_up, conv_w):
    d = D_MODEL
    tf = TF_UP
    nf = D_FF // tf
    return pl.pallas_call(
        _up_kernel,
        out_shape=jax.ShapeDtypeStruct((TOKENS, D_FF), BF16),
        grid=(BATCH, nf),
        in_specs=[
            pl.BlockSpec((SEQ, d), lambda i, j: (i, 0)),
            pl.BlockSpec((None, d, tf), lambda i, j: (l, 0, j)),
            pl.BlockSpec((None, d, tf), lambda i, j: (l, 0, nf + j)),
            pl.BlockSpec((None, 3, tf), lambda i, j: (l, 0, j)),
            pl.BlockSpec((None, 3, tf), lambda i, j: (l, 0, nf + j)),
        ],
        out_specs=pl.BlockSpec((SEQ, tf), lambda i, j: (i, j)),
        compiler_params=_cparams(("arbitrary", "arbitrary")),
        name="mlp_up",
    )(h, w_up, w_up, conv_w, conv_w)


def _down_kernel(act_ref, w_ref, x_ref, gt_ref, gpost_ref, *rest):
    b = pl.program_id(0) // (SEQ // TM_DOWN)
    y = jnp.dot(act_ref[...], w_ref[...], preferred_element_type=F32)
    ms = jnp.mean(y * y, axis=-1, keepdims=True)
    yn = y * lax.rsqrt(ms + RMS_EPS) * gpost_ref[...]
    x_new = x_ref[...] + gt_ref[pl.ds(b, 1), :] * yn
    if len(rest) == 1:
        (o_ref,) = rest
        o_ref[...] = x_new
    else:
        sh_ref, sc_ref, gnext_ref, o_ref, hn_ref = rest
        o_ref[...] = x_new
        hn_ref[...] = _norm_mod(x_new, gnext_ref[...], sc_ref[pl.ds(b, 1), :], sh_ref[pl.ds(b, 1), :])


def _mlp_down(l, act, w_down, x, mod, g_post3, g_mix3):
    d = D_MODEL
    tm = TM_DOWN
    last = l == DEPTH - 1
    tile = pl.BlockSpec((tm, d), lambda i: (i, 0))
    in_specs = [
        pl.BlockSpec((tm, D_FF), lambda i: (i, 0)),
        _resident((None, D_FF, d), lambda i: (l, 0, 0)),
        tile,
        pl.BlockSpec((None, BATCH, d), lambda i: (l, 0, 5)),
        _resident((None, 1, d), lambda i: (l, 0, 0)),
    ]
    args = [act, w_down, x, mod, g_post3]
    x_shape = jax.ShapeDtypeStruct((TOKENS, d), F32)
    if last:
        out_shape, out_specs = x_shape, tile
    else:
        in_specs += [
            pl.BlockSpec((None, BATCH, d), lambda i: (l + 1, 0, 0)),
            pl.BlockSpec((None, BATCH, d), lambda i: (l + 1, 0, 1)),
            _resident((None, 1, d), lambda i: (l + 1, 0, 0)),
        ]
        args += [mod, mod, g_mix3]
        out_shape, out_specs = (x_shape, jax.ShapeDtypeStruct((TOKENS, d), BF16)), (tile, tile)
    out = pl.pallas_call(
        _down_kernel,
        out_shape=out_shape,
        grid=(TOKENS // tm,),
        in_specs=in_specs,
        out_specs=out_specs,
        compiler_params=_cparams(("arbitrary",)),
        name="mlp_down",
    )(*args)
    return (out, None) if last else out


def _alibi_slopes():
    h = N_ATTN_HEADS
    return np.array([2.0 ** (-8.0 * (i + 1) / h) for i in range(h)], dtype=np.float32)


def kernel(x, c, w_mod, b_mod, g_pre_mix, g_post_mix, g_pre_ffn, g_post_ffn, w_in, ssm_log_dt, ssm_a_re,
           ssm_a_im, ssm_b_re, ssm_b_im, ssm_c_re, ssm_c_im, ssm_d, w_glu, b_glu, conv_mix_w, w_ssm_out,
           w_attn_out, w_conv_out, b_gate, w_o, w_up, ffn_conv_w, w_down):
    bias3 = _rows3(jnp.concatenate([jnp.zeros((DEPTH, OFF_GATE), F32), b_gate], axis=-1))
    w_glu_b = w_glu.astype(BF16)
    w_ssm_out_b = w_ssm_out.astype(BF16)
    w_attn_out_b = w_attn_out.astype(BF16)
    w_conv_out_b = w_conv_out.astype(BF16)
    w_o_b = w_o.astype(BF16)
    w_down_b = w_down.astype(BF16)
    w_up_b = w_up.astype(BF16)
    g_pre_mix3, g_post_mix3 = _rows3(g_pre_mix), _rows3(g_post_mix)
    g_pre_ffn3, g_post_ffn3 = _rows3(g_pre_ffn), _rows3(g_post_ffn)
    d_skip3, b_glu3 = _rows3(ssm_d), _rows3(b_glu)

    mod = _modulation(c, w_mod, b_mod)
    lr, li, bbr, bbi = _ssm_params(ssm_log_dt, ssm_a_re, ssm_a_im, ssm_b_re, ssm_b_im)
    slopes = jnp.asarray(_alibi_slopes())

    xf = x.reshape(TOKENS, D_MODEL)
    h = _first_norm(xf, g_pre_mix3, mod)
    for l in range(DEPTH):
        proj = _in_projection(l, h, w_in, bias3)
        bb, cre, cim, lam = _ssm_layout(lr[l], li[l], bbr[:, l], bbi[:, l], ssm_c_re[l], ssm_c_im[l])
        s = _ssm_mixer(l, proj.reshape(BATCH, SEQ, N_IN), bb, cre, cim, lam,
                       d_skip3, w_glu_b, b_glu3)
        a = _attention(proj, slopes)
        xf, h = _merge(l, s.reshape(TOKENS, SSM_WIDTH), a, proj, xf, mod, g_post_mix3, g_pre_ffn3, conv_mix_w,
                       w_ssm_out_b, w_attn_out_b, w_conv_out_b, w_o_b)
        act = _mlp_up(l, h, w_up_b, ffn_conv_w)
        xf, h = _mlp_down(l, act, w_down_b, xf, mod, g_post_ffn3, g_pre_mix3)
    return xf.reshape(BATCH, SEQ, D_MODEL)
```

```python
import numpy as np
import jax
import jax.numpy as jnp
from jax import lax
from jax.experimental import pallas as pl
from jax.experimental.pallas import tpu as pltpu

F32 = jnp.float32
BF16 = jnp.bfloat16

D_MODEL = 2048
BATCH = 8
SEQ = 2048
DEPTH = 4
TOKENS = BATCH * SEQ
RMS_EPS = 1e-6
NEG_INF = -1e30
N_BRANCH = 3
SSM_WIDTH = 512
SSM_GROUP = 16
SSM_GROUPS = 32
SSM_STATE = 64
HEAD_DIM = 64
DSWA_PATTERNS = ((128, 1), (512, 4), (2048, 16))
ATTN_WIDTH = 512
HEADS_PER_PATTERN = 8
N_ATTN_HEADS = 24
QKV_WIDTH = 1536
CONV_WIDTH = 512
D_FF = 5632
OFF_U = 0
OFF_Q = SSM_WIDTH
OFF_K = OFF_Q + QKV_WIDTH
OFF_V = OFF_K + QKV_WIDTH
OFF_CONV = OFF_V + QKV_WIDTH
OFF_GATE = OFF_CONV + 3 * CONV_WIDTH
N_IN = OFF_GATE + N_BRANCH * D_MODEL

LANES = 128
VMEM_LIMIT = 60 * 1024 * 1024

TM_NORM = 1024
TN_PROJ = 1280
TF_UP = 512
UP_HALO = 16
TM_MERGE = 256
CONV_HALO = 16
TN_MERGE = 512
TM_DOWN = 256
SSM_CHUNK = 128
QBLK = 128


def _cparams(sem):
    return pltpu.CompilerParams(dimension_semantics=sem, vmem_limit_bytes=VMEM_LIMIT)


def _resident(shape, index_map):
    return pl.BlockSpec(shape, index_map, pipeline_mode=pl.Buffered(1))


def _sigmoid(x):
    return 0.5 * jnp.tanh(0.5 * x) + 0.5


def _rows3(a):
    return a.reshape(DEPTH, 1, a.shape[-1])


def _norm_mod(x, g, scale_row, shift_row):
    ms = jnp.mean(x * x, axis=-1, keepdims=True)
    y = x * lax.rsqrt(ms + RMS_EPS) * g
    return (y * (1.0 + scale_row) + shift_row).astype(BF16)


def _mod_kernel(c_ref, w_ref, b_ref, o_ref):
    c = c_ref[...]
    cond = (c * jax.nn.sigmoid(c)).astype(BF16)
    o_ref[0] = jnp.dot(cond, w_ref[0].astype(BF16), preferred_element_type=F32) + b_ref[0]


def _modulation(c, w_mod, b_mod):
    tn = 1024
    n = 6 * D_MODEL
    return pl.pallas_call(
        _mod_kernel,
        out_shape=jax.ShapeDtypeStruct((DEPTH, BATCH, n), F32),
        grid=(DEPTH, n // tn),
        in_specs=[
            pl.BlockSpec((BATCH, D_MODEL), lambda l, j: (0, 0)),
            pl.BlockSpec((1, D_MODEL, tn), lambda l, j: (l, 0, j)),
            pl.BlockSpec((1, 1, tn), lambda l, j: (l, 0, j)),
        ],
        out_specs=pl.BlockSpec((1, BATCH, tn), lambda l, j: (l, 0, j)),
        compiler_params=_cparams(("arbitrary", "arbitrary")),
        name="modulation",
    )(c, w_mod, b_mod.reshape(DEPTH, 1, n))


def _norm_kernel(x_ref, g_ref, sh_ref, sc_ref, o_ref):
    b = pl.program_id(0) // (SEQ // TM_NORM)
    rows = 128

    def body(k, carry):
        r = pl.ds(pl.multiple_of(k * rows, rows), rows)
        o_ref[r, :] = _norm_mod(x_ref[r, :], g_ref[...], sc_ref[pl.ds(b, 1), :], sh_ref[pl.ds(b, 1), :])
        return carry

    lax.fori_loop(0, TM_NORM // rows, body, 0)


def _first_norm(x, g3, mod):
    d = D_MODEL
    return pl.pallas_call(
        _norm_kernel,
        out_shape=jax.ShapeDtypeStruct((TOKENS, d), BF16),
        grid=(TOKENS // TM_NORM,),
        in_specs=[
            pl.BlockSpec((TM_NORM, d), lambda i: (i, 0)),
            pl.BlockSpec((None, 1, d), lambda i: (0, 0, 0)),
            pl.BlockSpec((None, BATCH, d), lambda i: (0, 0, 0)),
            pl.BlockSpec((None, BATCH, d), lambda i: (0, 0, 1)),
        ],
        out_specs=pl.BlockSpec((TM_NORM, d), lambda i: (i, 0)),
        compiler_params=_cparams(("arbitrary",)),
        name="first_norm",
    )(x, g3, mod, mod)


def _inproj_kernel(h_ref, w_ref, bias_ref, o_ref):
    j = pl.program_id(1)
    raw_tiles = OFF_GATE // TN_PROJ
    raw_cols_in_mixed = OFF_GATE - raw_tiles * TN_PROJ

    @pl.when(j < raw_tiles)
    def _():
        o_ref[...] = jnp.dot(h_ref[...], w_ref[...].astype(BF16), preferred_element_type=F32).astype(BF16)

    @pl.when(j == raw_tiles)
    def _():
        acc = jnp.dot(h_ref[...], w_ref[...].astype(BF16), preferred_element_type=F32)
        col = lax.broadcasted_iota(jnp.int32, acc.shape, 1)
        o_ref[...] = jnp.where(col < raw_cols_in_mixed, acc, _sigmoid(acc + bias_ref[...])).astype(BF16)

    @pl.when(j > raw_tiles)
    def _():
        acc = jnp.dot(h_ref[...], w_ref[...].astype(BF16), preferred_element_type=F32)
        o_ref[...] = _sigmoid(acc + bias_ref[...]).astype(BF16)


def _in_projection(l, h, w_in, bias3):
    d = D_MODEL
    return pl.pallas_call(
        _inproj_kernel,
        out_shape=jax.ShapeDtypeStruct((TOKENS, N_IN), BF16),
        grid=(BATCH, N_IN // TN_PROJ),
        in_specs=[
            pl.BlockSpec((SEQ, d), lambda i, j: (i, 0)),
            pl.BlockSpec((None, d, TN_PROJ), lambda i, j: (l, 0, j)),
            pl.BlockSpec((None, 1, TN_PROJ), lambda i, j: (l, 0, j)),
        ],
        out_specs=pl.BlockSpec((SEQ, TN_PROJ), lambda i, j: (i, j)),
        compiler_params=_cparams(("arbitrary", "arbitrary")),
        name="in_projection",
    )(h, w_in, bias3)


def _ssm_param_kernel(ldt_ref, ar_ref, ai_ref, br_ref, bi_ref, lr_ref, li_ref, bbr_ref, bbi_ref):
    dt = jnp.exp(ldt_ref[...])
    ar = ar_ref[...]
    ai = ai_ref[...]
    mag = jnp.exp(ar * dt)
    lr = mag * jnp.cos(ai * dt)
    li = mag * jnp.sin(ai * dt)
    den = ar * ar + ai * ai
    fr = ((lr - 1.0) * ar + li * ai) / den
    fi = (li * ar - (lr - 1.0) * ai) / den
    lr_ref[...] = lr
    li_ref[...] = li
    for ch in range(SSM_GROUP):
        br = br_ref[ch]
        bi = bi_ref[ch]
        bbr_ref[ch] = fr * br - fi * bi
        bbi_ref[ch] = fr * bi + fi * br


def _ssm_params(log_dt, a_re, a_im, b_re, b_im):
    n = DEPTH * SSM_GROUPS
    mat = jax.ShapeDtypeStruct((n, SSM_STATE), F32)
    cube = jax.ShapeDtypeStruct((SSM_GROUP, n, SSM_STATE), F32)

    def channel_major(b):
        return b.transpose(3, 0, 1, 2).reshape(SSM_GROUP, n, SSM_STATE)

    lr, li, bbr, bbi = pl.pallas_call(
        _ssm_param_kernel,
        out_shape=(mat, mat, cube, cube),
        name="ssm_params",
    )(log_dt.reshape(n, 1), a_re.reshape(n, SSM_STATE), a_im.reshape(n, SSM_STATE),
      channel_major(b_re), channel_major(b_im))
    shape3 = (DEPTH, SSM_GROUPS, SSM_STATE)
    return (lr.reshape(shape3), li.reshape(shape3),
            bbr.reshape((SSM_GROUP,) + shape3), bbi.reshape((SSM_GROUP,) + shape3))


N_SLAB = SSM_WIDTH // LANES
GROUPS_PER_SLAB = LANES // SSM_GROUP
STATES_PER_SLAB = GROUPS_PER_SLAB * SSM_STATE
N_STATES = SSM_GROUPS * SSM_STATE


def _ssm_layout(lr, li, bbr, bbi, c_re, c_im):
    eye = jnp.eye(GROUPS_PER_SLAB, dtype=F32)

    def in_blocks(bb):
        t = bb.reshape(SSM_GROUP, N_SLAB, GROUPS_PER_SLAB, SSM_STATE).transpose(1, 2, 0, 3)
        blk = eye[None, :, None, :, None] * t[:, :, :, None, :]
        return blk.reshape(N_SLAB, LANES, STATES_PER_SLAB)

    def out_blocks(cc):
        t = cc.reshape(N_SLAB, GROUPS_PER_SLAB, SSM_GROUP, SSM_STATE).transpose(0, 1, 3, 2)
        blk = eye[None, :, None, :, None] * t[:, :, :, None, :]
        return blk.reshape(N_SLAB, STATES_PER_SLAB, LANES)

    bb = jnp.concatenate([in_blocks(bbr), in_blocks(bbi)], axis=-1).astype(BF16)
    lam = jnp.stack([lr.reshape(N_STATES), li.reshape(N_STATES)], axis=0)
    return bb, out_blocks(c_re).astype(BF16), out_blocks(c_im).astype(BF16), lam


def _ssm_kernel(u_ref, bb_ref, cre_ref, cim_ref, lam_ref, dsk_ref, wglu_ref, bglu_ref,
                o_ref, x_scr, h_scr, *stage):
    u_stage, o_stage = stage[:N_SLAB], stage[N_SLAB:]

    @pl.when(pl.program_id(0) == 0)
    def _():
        h_scr[...] = jnp.zeros_like(h_scr)

    for b in range(BATCH):
        u_b = u_ref[b].astype(F32)
        for s in range(N_SLAB):
            u_stage[s][pl.ds(b, SSM_CHUNK, stride=BATCH), :] = u_b[:, s * LANES:(s + 1) * LANES]
    u_tm = jnp.concatenate([u_stage[s][...] for s in range(N_SLAB)], axis=-1)
    u_tm_b = u_tm.astype(BF16)

    for s in range(N_SLAB):
        xs = jnp.dot(u_tm_b[:, s * LANES:(s + 1) * LANES], bb_ref[s], preferred_element_type=F32)
        x_scr[:, s * STATES_PER_SLAB:(s + 1) * STATES_PER_SLAB] = xs[:, :STATES_PER_SLAB]
        x_scr[:, N_STATES + s * STATES_PER_SLAB:N_STATES + (s + 1) * STATES_PER_SLAB] = xs[:, STATES_PER_SLAB:]

    width = STATES_PER_SLAB
    for s in range(N_STATES // width):
        re_cols = pl.ds(s * width, width)
        im_cols = pl.ds(N_STATES + s * width, width)
        lr = jnp.broadcast_to(lam_ref[0:1, s * width:(s + 1) * width], (BATCH, width))
        li = jnp.broadcast_to(lam_ref[1:2, s * width:(s + 1) * width], (BATCH, width))

        def step(t, carry, re_cols=re_cols, im_cols=im_cols, lr=lr, li=li):
            hr, hi = carry
            r0 = pl.multiple_of(t * BATCH, BATCH)
            xr = x_scr[pl.ds(r0, BATCH), re_cols]
            xi = x_scr[pl.ds(r0, BATCH), im_cols]
            nr = lr * hr - li * hi + xr
            ni = lr * hi + li * hr + xi
            x_scr[pl.ds(r0, BATCH), re_cols] = nr
            x_scr[pl.ds(r0, BATCH), im_cols] = ni
            return nr, ni

        hr, hi = lax.fori_loop(0, SSM_CHUNK, step, (h_scr[:, re_cols], h_scr[:, im_cols]), unroll=4)
        h_scr[:, re_cols] = hr
        h_scr[:, im_cols] = hi

    ys = []
    for s in range(N_SLAB):
        h_re = x_scr[:, s * STATES_PER_SLAB:(s + 1) * STATES_PER_SLAB].astype(BF16)
        h_im = x_scr[:, N_STATES + s * STATES_PER_SLAB:N_STATES + (s + 1) * STATES_PER_SLAB].astype(BF16)
        ys.append(jnp.dot(h_re, cre_ref[s], preferred_element_type=F32)
                  - jnp.dot(h_im, cim_ref[s], preferred_element_type=F32))
    y = jnp.concatenate(ys, axis=-1) + dsk_ref[...] * u_tm
    g = jax.nn.gelu(y)
    gate = jnp.dot(g.astype(BF16), wglu_ref[...], preferred_element_type=F32) + bglu_ref[...]
    out_tm = g * _sigmoid(gate)
    for s in range(N_SLAB):
        o_stage[s][...] = out_tm[:, s * LANES:(s + 1) * LANES]
    for b in range(BATCH):
        rows_b = [o_stage[s][pl.ds(b, SSM_CHUNK, stride=BATCH), :] for s in range(N_SLAB)]
        o_ref[b] = jnp.concatenate(rows_b, axis=-1).astype(BF16)


def _ssm_mixer(l, proj3, bb, cre, cim, lam, d_skip3, w_glu, b_glu3):
    rows = BATCH * SSM_CHUNK
    return pl.pallas_call(
        _ssm_kernel,
        out_shape=jax.ShapeDtypeStruct((BATCH, SEQ, SSM_WIDTH), BF16),
        grid=(SEQ // SSM_CHUNK,),
        in_specs=[
            pl.BlockSpec((BATCH, SSM_CHUNK, SSM_WIDTH), lambda c: (0, c, OFF_U // SSM_WIDTH)),
            _resident((N_SLAB, LANES, 2 * STATES_PER_SLAB), lambda c: (0, 0, 0)),
            _resident((N_SLAB, STATES_PER_SLAB, LANES), lambda c: (0, 0, 0)),
            _resident((N_SLAB, STATES_PER_SLAB, LANES), lambda c: (0, 0, 0)),
            _resident((2, N_STATES), lambda c: (0, 0)),
            _resident((None, 1, SSM_WIDTH), lambda c: (l, 0, 0)),
            _resident((None, SSM_WIDTH, SSM_WIDTH), lambda c: (l, 0, 0)),
            _resident((None, 1, SSM_WIDTH), lambda c: (l, 0, 0)),
        ],
        out_specs=pl.BlockSpec((BATCH, SSM_CHUNK, SSM_WIDTH), lambda c: (0, c, 0)),
        scratch_shapes=[pltpu.VMEM((rows, 2 * N_STATES), F32), pltpu.VMEM((BATCH, 2 * N_STATES), F32)]
        + [pltpu.VMEM((rows, LANES), F32)] * (2 * N_SLAB),
        compiler_params=_cparams(("arbitrary",)),
        name="ssm_mixer",
    )(proj3, bb, cre, cim, lam, d_skip3, w_glu, b_glu3)


ATTN_UNROLL_LATER = {1: 5, 4: 6}
ATTN_UNROLL_FIRST = 8


def _attn_kernel(slopes_ref, q0, k0, v0, q1, k1, v1, q2, k2, v2, o_ref,
                 qp, kt, va, vb, o0, l0, o1, l1, o2, l2):
    pair = pl.program_id(1)
    n_qblk = SEQ // QBLK
    lane = lax.broadcasted_iota(jnp.int32, (QBLK, LANES), 1)
    head0 = lane < HEAD_DIM
    qi = lax.broadcasted_iota(jnp.int32, (QBLK, 2 * QBLK), 0)
    kj = lax.broadcasted_iota(jnp.int32, (QBLK, 2 * QBLK), 1)
    dist = QBLK + qi - kj
    valid = (dist >= 0) & (dist <= QBLK)
    dist = dist.astype(F32)

    groups = ((q0, k0, v0, o0, l0), (q1, k1, v1, o1, l1), (q2, k2, v2, o2, l2))
    for g, (_, dil) in enumerate(DSWA_PATTERNS):
        q_ref, k_ref, v_ref, o_scr, l_scr = groups[g]
        n_blocks = SEQ // (dil * QBLK)

        def rows(start, dil=dil):
            if dil == 1:
                return pl.ds(pl.multiple_of(start, QBLK), QBLK)
            return pl.ds(start, QBLK, stride=dil)

        def natural_rows(idx, rows=rows, n_blocks=n_blocks, dil=dil):
            return rows(dil * QBLK * (idx % n_blocks) + idx // n_blocks)

        def reorder(idx, gather, carry=0):
            dst = pl.ds(pl.multiple_of(idx * QBLK, QBLK), QBLK)
            qp[dst, :] = (gather(q_ref) * (HEAD_DIM ** -0.5)).astype(BF16)
            kt[idx] = gather(k_ref).T.astype(BF16)
            v = gather(v_ref)
            va[dst, :] = jnp.where(head0, v, 1.0).astype(BF16)
            vb[dst, :] = jnp.where(head0, 1.0, v).astype(BF16)
            return carry

        if dil == 1:
            def reorder_natural(idx, carry, reorder=reorder):
                src = pl.ds(pl.multiple_of(idx * QBLK, QBLK), QBLK)
                return reorder(idx, lambda ref: ref[src, :].astype(F32), carry)

            lax.fori_loop(0, n_qblk, reorder_natural, 0, unroll=4)
        else:
            for idx in range(n_qblk):
                first_token = dil * QBLK * (idx % n_blocks) + idx // n_blocks

                def gather_strided(ref, first_token=first_token, dil=dil):
                    words = ref.bitcast(jnp.uint32)[pl.ds(first_token // 2, QBLK, stride=dil // 2), :]
                    bits = (words << 16) if first_token % 2 == 0 else (words & jnp.uint32(0xFFFF0000))
                    return pltpu.bitcast(bits, F32)

                reorder(idx, gather_strided)

        bias = []
        for hh in range(2):
            slope = slopes_ref[g * HEADS_PER_PATTERN + 2 * pair + hh] * float(dil)
            bias.append(jnp.where(valid, -slope * dist, NEG_INF))

        def block(idx, with_prev, o_scr=o_scr, l_scr=l_scr, bias=bias, natural_rows=natural_rows):
            cur = pl.ds(pl.multiple_of(idx * QBLK, QBLK), QBLK)
            q = qp[cur, :]
            if with_prev:
                keys_t = jnp.concatenate([kt[idx - 1], kt[idx]], axis=1)
                kv_rows = pl.ds(pl.multiple_of((idx - 1) * QBLK, QBLK), 2 * QBLK)
            else:
                keys_t = kt[idx]
                kv_rows = cur
            pvs, ms = [], []
            for hh, v_scr in ((0, va), (1, vb)):
                qh = jnp.where(head0 == (hh == 0), q, jnp.zeros_like(q))
                s = jnp.dot(qh, keys_t, preferred_element_type=F32)
                s = s + (bias[hh] if with_prev else bias[hh][:, QBLK:])
                m = jnp.max(s, axis=-1, keepdims=True)
                p = jnp.exp(s - m).astype(BF16)
                pvs.append(jnp.dot(p, v_scr[kv_rows, :], preferred_element_type=F32))
                ms.append(m)
            num = jnp.where(head0, pvs[0], pvs[1])
            den = pltpu.roll(jnp.where(head0, pvs[1], pvs[0]), HEAD_DIM, axis=1)
            dst = natural_rows(idx)
            o_scr[dst, :] = num / den
            l_scr[dst, :] = jnp.where(head0, ms[0], ms[1]) + jnp.log(den)

        def first_blocks(res, carry, block=block, n_blocks=n_blocks):
            block(res * n_blocks, False)
            return carry

        lax.fori_loop(0, dil, first_blocks, 0, unroll=min(dil, ATTN_UNROLL_FIRST))

        if n_blocks > 1:
            later = n_blocks - 1

            def later_blocks(k, carry, block=block, later=later, n_blocks=n_blocks):
                block((k // later) * n_blocks + k % later + 1, True)
                return carry

            lax.fori_loop(0, dil * later, later_blocks, 0, unroll=ATTN_UNROLL_LATER[dil])

    def merge(k, carry):
        r = pl.ds(pl.multiple_of(k * QBLK, QBLK), QBLK)
        la, lb, lc = l0[r, :], l1[r, :], l2[r, :]
        m = jnp.maximum(jnp.maximum(la, lb), lc)
        wa, wb, wc = jnp.exp(la - m), jnp.exp(lb - m), jnp.exp(lc - m)
        tot = wa + wb + wc
        o_ref[r, :] = ((wa * o0[r, :] + wb * o1[r, :] + wc * o2[r, :]) / tot).astype(BF16)
        return carry

    lax.fori_loop(0, n_qblk, merge, 0, unroll=2)


def _attention(proj, slopes):
    pairs = HEADS_PER_PATTERN // 2

    def col(base, g):
        return lambda b, p: (b, (base + g * HEADS_PER_PATTERN * HEAD_DIM) // LANES + p)

    in_specs = [pl.BlockSpec(memory_space=pltpu.SMEM)]
    for g in range(len(DSWA_PATTERNS)):
        for base in (OFF_Q, OFF_K, OFF_V):
            in_specs.append(pl.BlockSpec((SEQ, LANES), col(base, g)))
    scr = pltpu.VMEM((SEQ, LANES), F32)
    seq_b = pltpu.VMEM((SEQ, LANES), BF16)
    kt_b = pltpu.VMEM((SEQ // QBLK, LANES, QBLK), BF16)
    return pl.pallas_call(
        _attn_kernel,
        out_shape=jax.ShapeDtypeStruct((TOKENS, ATTN_WIDTH), BF16),
        grid=(BATCH, pairs),
        in_specs=in_specs,
        out_specs=pl.BlockSpec((SEQ, LANES), lambda b, p: (b, p)),
        scratch_shapes=[seq_b, kt_b, seq_b, seq_b] + [scr] * 6,
        compiler_params=_cparams(("arbitrary", "arbitrary")),
        name="dilated_attention",
    )(slopes, *([proj] * 9))


N_MERGE_CHUNKS = D_MODEL // TN_MERGE


def _merge_kernel(*refs):
    (s_ref, a_ref, cb_ref, cc_ref, ch_ref, cch_ref, chh_ref), refs = refs[:7], refs[7:]
    gate_refs, refs = refs[:N_BRANCH * N_MERGE_CHUNKS], refs[N_BRANCH * N_MERGE_CHUNKS:]
    (x_ref, gt_ref, sh2_ref, sc2_ref, gpost_ref, gffn_ref, cw_ref, wss_ref, wat_ref, wcv_ref, wo_ref,
     o_ref, h2_ref, z_scr, m_scr) = refs
    i = pl.program_id(0)
    tiles_per_seq = SEQ // TM_MERGE
    b = i // tiles_per_seq
    seq_start = (i % tiles_per_seq) == 0

    halo = cch_ref[...].astype(F32) * chh_ref[...].astype(F32)
    z_scr[0:CONV_HALO, :] = jnp.where(seq_start, 0.0, halo)
    z_scr[CONV_HALO:CONV_HALO + TM_MERGE, :] = cc_ref[...].astype(F32) * ch_ref[...].astype(F32)
    conv = (cw_ref[0:1, :] * z_scr[pl.ds(CONV_HALO, TM_MERGE), :]
            + cw_ref[1:2, :] * z_scr[pl.ds(CONV_HALO - 1, TM_MERGE), :]
            + cw_ref[2:3, :] * z_scr[pl.ds(CONV_HALO - 2, TM_MERGE), :])
    cv = (cb_ref[...].astype(F32) * conv).astype(BF16)

    s = s_ref[...]
    a = a_ref[...]
    for c in range(N_MERGE_CHUNKS):
        cols = slice(c * TN_MERGE, (c + 1) * TN_MERGE)
        y_ssm = jnp.dot(s, wss_ref[:, cols], preferred_element_type=F32)
        y_att = jnp.dot(a, wat_ref[:, cols], preferred_element_type=F32)
        y_cv = jnp.dot(cv, wcv_ref[:, cols], preferred_element_type=F32)
        merged = (gate_refs[c][...].astype(F32) * y_ssm
                  + gate_refs[N_MERGE_CHUNKS + c][...].astype(F32) * y_att
                  + gate_refs[2 * N_MERGE_CHUNKS + c][...].astype(F32) * y_cv)
        m_scr[:, cols] = merged.astype(BF16)

    y = jnp.dot(m_scr[...], wo_ref[...], preferred_element_type=F32)
    ms = jnp.mean(y * y, axis=-1, keepdims=True)
    yn = y * lax.rsqrt(ms + RMS_EPS) * gpost_ref[...]
    x_new = x_ref[...] + gt_ref[pl.ds(b, 1), :] * yn
    o_ref[...] = x_new
    h2_ref[...] = _norm_mod(x_new, gffn_ref[...], sc2_ref[pl.ds(b, 1), :], sh2_ref[pl.ds(b, 1), :])


def _merge(l, s, a, proj, x, mod, g_post3, g_ffn3, conv_w, w_ssm_out, w_attn_out, w_conv_out, w_o):
    tm = TM_MERGE
    d = D_MODEL
    cw = CONV_WIDTH
    cb_blk = OFF_CONV // cw
    halo_blocks = tm // CONV_HALO

    def halo_map(col_blk):
        return lambda i: (jnp.maximum(i * halo_blocks - 1, 0), col_blk)

    def gate_spec(branch, chunk):
        blk = (OFF_GATE + branch * d) // TN_MERGE + chunk
        return pl.BlockSpec((tm, TN_MERGE), lambda i: (i, blk))

    gate_specs = [gate_spec(br, c) for br in range(N_BRANCH) for c in range(N_MERGE_CHUNKS)]
    tile = pl.BlockSpec((tm, d), lambda i: (i, 0))
    return pl.pallas_call(
        _merge_kernel,
        out_shape=(jax.ShapeDtypeStruct((TOKENS, d), F32), jax.ShapeDtypeStruct((TOKENS, d), BF16)),
        grid=(TOKENS // tm,),
        in_specs=[
            pl.BlockSpec((tm, SSM_WIDTH), lambda i: (i, 0)),
            pl.BlockSpec((tm, ATTN_WIDTH), lambda i: (i, 0)),
            pl.BlockSpec((tm, cw), lambda i: (i, cb_blk)),
            pl.BlockSpec((tm, cw), lambda i: (i, cb_blk + 1)),
            pl.BlockSpec((tm, cw), lambda i: (i, cb_blk + 2)),
            pl.BlockSpec((CONV_HALO, cw), halo_map(cb_blk + 1)),
            pl.BlockSpec((CONV_HALO, cw), halo_map(cb_blk + 2)),
        ] + gate_specs + [
            tile,
            pl.BlockSpec((None, BATCH, d), lambda i: (l, 0, 2)),
            pl.BlockSpec((None, BATCH, d), lambda i: (l, 0, 3)),
            pl.BlockSpec((None, BATCH, d), lambda i: (l, 0, 4)),
            _resident((None, 1, d), lambda i: (l, 0, 0)),
            _resident((None, 1, d), lambda i: (l, 0, 0)),
            _resident((None, 3, cw), lambda i: (l, 0, 0)),
            _resident((None, SSM_WIDTH, d), lambda i: (l, 0, 0)),
            _resident((None, ATTN_WIDTH, d), lambda i: (l, 0, 0)),
            _resident((None, cw, d), lambda i: (l, 0, 0)),
            _resident((None, d, d), lambda i: (l, 0, 0)),
        ],
        out_specs=(tile, tile),
        scratch_shapes=[pltpu.VMEM((tm + CONV_HALO, cw), F32), pltpu.VMEM((tm, d), BF16)],
        compiler_params=_cparams(("arbitrary",)),
        name="branch_merge",
    )(s, a, *([proj] * (5 + len(gate_specs))), x, mod, mod, mod, g_post3, g_ffn3, conv_w,
      w_ssm_out, w_attn_out, w_conv_out, w_o)


def _up_kernel(h_ref, wa_ref, wb_ref, cwa_ref, cwb_ref, o_ref):
    half = SEQ // 2
    row = lax.broadcasted_iota(jnp.int32, (half, 1), 0)
    for first in (True, False):
        lead = 0 if first else UP_HALO
        h = h_ref[pl.ds(0 if first else half - UP_HALO, half + lead), :]
        convs = []
        for w_ref, cw_ref in ((wa_ref, cwa_ref), (wb_ref, cwb_ref)):
            z = jnp.dot(h, w_ref[...].astype(BF16), preferred_element_type=F32)
            z1 = pltpu.roll(z, 1, axis=0)
            z2 = pltpu.roll(z, 2, axis=0)
            if first:
                z1 = jnp.where(row >= 1, z1, 0.0)
                z2 = jnp.where(row >= 2, z2, 0.0)
            y = cw_ref[0:1, :] * z + cw_ref[1:2, :] * z1 + cw_ref[2:3, :] * z2
            convs.append(y[lead:, :])
        ca, cb = convs
        o_ref[pl.ds(0 if first else half, half), :] = (ca * _sigmoid(ca) * cb).astype(BF16)


def _mlp_up(l, h, w_up, conv_w):
    d = D_MODEL
    tf = TF_UP
    nf = D_FF // tf
    return pl.pallas_call(
        _up_kernel,
        out_shape=jax.ShapeDtypeStruct((TOKENS, D_FF), BF16),
        grid=(BATCH, nf),
        in_specs=[
            pl.BlockSpec((SEQ, d), lambda i, j: (i, 0)),
            pl.BlockSpec((None, d, tf), lambda i, j: (l, 0, j)),
            pl.BlockSpec((None, d, tf), lambda i, j: (l, 0, nf + j)),
            pl.BlockSpec((None, 3, tf), lambda i, j: (l, 0, j)),
            pl.BlockSpec((None, 3, tf), lambda i, j: (l, 0, nf + j)),
        ],
        out_specs=pl.BlockSpec((SEQ, tf), lambda i, j: (i, j)),
        compiler_params=_cparams(("arbitrary", "arbitrary")),
        name="mlp_up",
    )(h, w_up, w_up, conv_w, conv_w)


def _down_kernel(act_ref, w_ref, x_ref, gt_ref, gpost_ref, *rest):
    b = pl.program_id(0) // (SEQ // TM_DOWN)
    y = jnp.dot(act_ref[...], w_ref[...], preferred_element_type=F32)
    ms = jnp.mean(y * y, axis=-1, keepdims=True)
    yn = y * lax.rsqrt(ms + RMS_EPS) * gpost_ref[...]
    x_new = x_ref[...] + gt_ref[pl.ds(b, 1), :] * yn
    if len(rest) == 1:
        (o_ref,) = rest
        o_ref[...] = x_new
    else:
        sh_ref, sc_ref, gnext_ref, o_ref, hn_ref = rest
        o_ref[...] = x_new
        hn_ref[...] = _norm_mod(x_new, gnext_ref[...], sc_ref[pl.ds(b, 1), :], sh_ref[pl.ds(b, 1), :])


def _mlp_down(l, act, w_down, x, mod, g_post3, g_mix3):
    d = D_MODEL
    tm = TM_DOWN
    last = l == DEPTH - 1
    tile = pl.BlockSpec((tm, d), lambda i: (i, 0))
    in_specs = [
        pl.BlockSpec((tm, D_FF), lambda i: (i, 0)),
        _resident((None, D_FF, d), lambda i: (l, 0, 0)),
        tile,
        pl.BlockSpec((None, BATCH, d), lambda i: (l, 0, 5)),
        _resident((None, 1, d), lambda i: (l, 0, 0)),
    ]
    args = [act, w_down, x, mod, g_post3]
    x_shape = jax.ShapeDtypeStruct((TOKENS, d), F32)
    if last:
        out_shape, out_specs = x_shape, tile
    else:
        in_specs += [
            pl.BlockSpec((None, BATCH, d), lambda i: (l + 1, 0, 0)),
            pl.BlockSpec((None, BATCH, d), lambda i: (l + 1, 0, 1)),
            _resident((None, 1, d), lambda i: (l + 1, 0, 0)),
        ]
        args += [mod, mod, g_mix3]
        out_shape, out_specs = (x_shape, jax.ShapeDtypeStruct((TOKENS, d), BF16)), (tile, tile)
    out = pl.pallas_call(
        _down_kernel,
        out_shape=out_shape,
        grid=(TOKENS // tm,),
        in_specs=in_specs,
        out_specs=out_specs,
        compiler_params=_cparams(("arbitrary",)),
        name="mlp_down",
    )(*args)
    return (out, None) if last else out


def _alibi_slopes():
    h = N_ATTN_HEADS
    return np.array([2.0 ** (-8.0 * (i + 1) / h) for i in range(h)], dtype=np.float32)


def kernel(x, c, w_mod, b_mod, g_pre_mix, g_post_mix, g_pre_ffn, g_post_ffn, w_in, ssm_log_dt, ssm_a_re,
           ssm_a_im, ssm_b_re, ssm_b_im, ssm_c_re, ssm_c_im, ssm_d, w_glu, b_glu, conv_mix_w, w_ssm_out,
           w_attn_out, w_conv_out, b_gate, w_o, w_up, ffn_conv_w, w_down):
    bias3 = _rows3(jnp.concatenate([jnp.zeros((DEPTH, OFF_GATE), F32), b_gate], axis=-1))
    w_glu_b = w_glu.astype(BF16)
    w_ssm_out_b = w_ssm_out.astype(BF16)
    w_attn_out_b = w_attn_out.astype(BF16)
    w_conv_out_b = w_conv_out.astype(BF16)
    w_o_b = w_o.astype(BF16)
    w_down_b = w_down.astype(BF16)
    g_pre_mix3, g_post_mix3 = _rows3(g_pre_mix), _rows3(g_post_mix)
    g_pre_ffn3, g_post_ffn3 = _rows3(g_pre_ffn), _rows3(g_post_ffn)
    d_skip3, b_glu3 = _rows3(ssm_d), _rows3(b_glu)

    mod = _modulation(c, w_mod, b_mod)
    lr, li, bbr, bbi = _ssm_params(ssm_log_dt, ssm_a_re, ssm_a_im, ssm_b_re, ssm_b_im)
    slopes = jnp.asarray(_alibi_slopes())

    xf = x.reshape(TOKENS, D_MODEL)
    h = _first_norm(xf, g_pre_mix3, mod)
    for l in range(DEPTH):
        proj = _in_projection(l, h, w_in, bias3)
        bb, cre, cim, lam = _ssm_layout(lr[l], li[l], bbr[:, l], bbi[:, l], ssm_c_re[l], ssm_c_im[l])
        s = _ssm_mixer(l, proj.reshape(BATCH, SEQ, N_IN), bb, cre, cim, lam,
                       d_skip3, w_glu_b, b_glu3)
        a = _attention(proj, slopes)
        xf, h = _merge(l, s.reshape(TOKENS, SSM_WIDTH), a, proj, xf, mod, g_post_mix3, g_pre_ffn3, conv_mix_w,
                       w_ssm_out_b, w_attn_out_b, w_conv_out_b, w_o_b)
        act = _mlp_up(l, h, w_up, ffn_conv_w)
        xf, h = _mlp_down(l, act, w_down_b, xf, mod, g_post_ffn3, g_pre_mix3)
    return xf.reshape(BATCH, SEQ, D_MODEL)
```

```python
import numpy as np
import jax
import jax.numpy as jnp
from jax import lax
from jax.experimental import pallas as pl
from jax.experimental.pallas import tpu as pltpu

F32 = jnp.float32
BF16 = jnp.bfloat16

D_MODEL = 2048
BATCH = 8
SEQ = 2048
DEPTH = 4
TOKENS = BATCH * SEQ
RMS_EPS = 1e-6
NEG_INF = -1e30
N_BRANCH = 3
SSM_WIDTH = 512
SSM_GROUP = 16
SSM_GROUPS = 32
SSM_STATE = 64
HEAD_DIM = 64
DSWA_PATTERNS = ((128, 1), (512, 4), (2048, 16))
ATTN_WIDTH = 512
HEADS_PER_PATTERN = 8
N_ATTN_HEADS = 24
QKV_WIDTH = 1536
CONV_WIDTH = 512
D_FF = 5632
OFF_U = 0
OFF_Q = SSM_WIDTH
OFF_K = OFF_Q + QKV_WIDTH
OFF_V = OFF_K + QKV_WIDTH
OFF_CONV = OFF_V + QKV_WIDTH
OFF_GATE = OFF_CONV + 3 * CONV_WIDTH
N_IN = OFF_GATE + N_BRANCH * D_MODEL

LANES = 128
VMEM_LIMIT = 62 * 1024 * 1024

TM_NORM = 1024
TN_PROJ = 1280
TF_UP = 512
UP_HALO = 16
TM_MERGE = 512
MERGE_SUBTILES = 2
CONV_HALO = 16
TN_MERGE = 512
TM_DOWN = 512
DOWN_SUBTILES = 2
SSM_CHUNK = 128
QBLK = 128


def _cparams(sem):
    return pltpu.CompilerParams(dimension_semantics=sem, vmem_limit_bytes=VMEM_LIMIT)


def _resident(shape, index_map):
    return pl.BlockSpec(shape, index_map, pipeline_mode=pl.Buffered(1))


def _sigmoid(x):
    return 0.5 * jnp.tanh(0.5 * x) + 0.5


def _rows3(a):
    return a.reshape(DEPTH, 1, a.shape[-1])


def _norm_mod(x, g, scale_row, shift_row):
    ms = jnp.mean(x * x, axis=-1, keepdims=True)
    y = x * lax.rsqrt(ms + RMS_EPS) * g
    return (y * (1.0 + scale_row) + shift_row).astype(BF16)


def _mod_kernel(c_ref, w_ref, b_ref, o_ref):
    c = c_ref[...]
    cond = (c * jax.nn.sigmoid(c)).astype(BF16)
    o_ref[0] = jnp.dot(cond, w_ref[0].astype(BF16), preferred_element_type=F32) + b_ref[0]


def _modulation(c, w_mod, b_mod):
    tn = 1024
    n = 6 * D_MODEL
    return pl.pallas_call(
        _mod_kernel,
        out_shape=jax.ShapeDtypeStruct((DEPTH, BATCH, n), F32),
        grid=(DEPTH, n // tn),
        in_specs=[
            pl.BlockSpec((BATCH, D_MODEL), lambda l, j: (0, 0)),
            pl.BlockSpec((1, D_MODEL, tn), lambda l, j: (l, 0, j)),
            pl.BlockSpec((1, 1, tn), lambda l, j: (l, 0, j)),
        ],
        out_specs=pl.BlockSpec((1, BATCH, tn), lambda l, j: (l, 0, j)),
        compiler_params=_cparams(("arbitrary", "arbitrary")),
        name="modulation",
    )(c, w_mod, b_mod.reshape(DEPTH, 1, n))


def _norm_kernel(x_ref, g_ref, sh_ref, sc_ref, o_ref):
    b = pl.program_id(0) // (SEQ // TM_NORM)
    rows = 128

    def body(k, carry):
        r = pl.ds(pl.multiple_of(k * rows, rows), rows)
        o_ref[r, :] = _norm_mod(x_ref[r, :], g_ref[...], sc_ref[pl.ds(b, 1), :], sh_ref[pl.ds(b, 1), :])
        return carry

    lax.fori_loop(0, TM_NORM // rows, body, 0)


def _first_norm(x, g3, mod):
    d = D_MODEL
    return pl.pallas_call(
        _norm_kernel,
        out_shape=jax.ShapeDtypeStruct((TOKENS, d), BF16),
        grid=(TOKENS // TM_NORM,),
        in_specs=[
            pl.BlockSpec((TM_NORM, d), lambda i: (i, 0)),
            pl.BlockSpec((None, 1, d), lambda i: (0, 0, 0)),
            pl.BlockSpec((None, BATCH, d), lambda i: (0, 0, 0)),
            pl.BlockSpec((None, BATCH, d), lambda i: (0, 0, 1)),
        ],
        out_specs=pl.BlockSpec((TM_NORM, d), lambda i: (i, 0)),
        compiler_params=_cparams(("arbitrary",)),
        name="first_norm",
    )(x, g3, mod, mod)


def _inproj_kernel(h_ref, w_ref, bias_ref, o_ref):
    j = pl.program_id(1)
    raw_tiles = OFF_GATE // TN_PROJ
    raw_cols_in_mixed = OFF_GATE - raw_tiles * TN_PROJ

    @pl.when(j < raw_tiles)
    def _():
        o_ref[...] = jnp.dot(h_ref[...], w_ref[...].astype(BF16), preferred_element_type=F32).astype(BF16)

    @pl.when(j == raw_tiles)
    def _():
        acc = jnp.dot(h_ref[...], w_ref[...].astype(BF16), preferred_element_type=F32)
        col = lax.broadcasted_iota(jnp.int32, acc.shape, 1)
        o_ref[...] = jnp.where(col < raw_cols_in_mixed, acc, _sigmoid(acc + bias_ref[...])).astype(BF16)

    @pl.when(j > raw_tiles)
    def _():
        acc = jnp.dot(h_ref[...], w_ref[...].astype(BF16), preferred_element_type=F32)
        o_ref[...] = _sigmoid(acc + bias_ref[...]).astype(BF16)


def _in_projection(l, h, w_in, bias3):
    d = D_MODEL
    return pl.pallas_call(
        _inproj_kernel,
        out_shape=jax.ShapeDtypeStruct((TOKENS, N_IN), BF16),
        grid=(BATCH, N_IN // TN_PROJ),
        in_specs=[
            pl.BlockSpec((SEQ, d), lambda i, j: (i, 0)),
            pl.BlockSpec((None, d, TN_PROJ), lambda i, j: (l, 0, j)),
            pl.BlockSpec((None, 1, TN_PROJ), lambda i, j: (l, 0, j)),
        ],
        out_specs=pl.BlockSpec((SEQ, TN_PROJ), lambda i, j: (i, j)),
        compiler_params=_cparams(("arbitrary", "arbitrary")),
        name="in_projection",
    )(h, w_in, bias3)


def _ssm_param_kernel(ldt_ref, ar_ref, ai_ref, br_ref, bi_ref, lr_ref, li_ref, bbr_ref, bbi_ref):
    dt = jnp.exp(ldt_ref[...])
    ar = ar_ref[...]
    ai = ai_ref[...]
    mag = jnp.exp(ar * dt)
    lr = mag * jnp.cos(ai * dt)
    li = mag * jnp.sin(ai * dt)
    den = ar * ar + ai * ai
    fr = ((lr - 1.0) * ar + li * ai) / den
    fi = (li * ar - (lr - 1.0) * ai) / den
    lr_ref[...] = lr
    li_ref[...] = li
    for ch in range(SSM_GROUP):
        br = br_ref[ch]
        bi = bi_ref[ch]
        bbr_ref[ch] = fr * br - fi * bi
        bbi_ref[ch] = fr * bi + fi * br


def _ssm_params(log_dt, a_re, a_im, b_re, b_im):
    n = DEPTH * SSM_GROUPS
    mat = jax.ShapeDtypeStruct((n, SSM_STATE), F32)
    cube = jax.ShapeDtypeStruct((SSM_GROUP, n, SSM_STATE), F32)

    def channel_major(b):
        return b.transpose(3, 0, 1, 2).reshape(SSM_GROUP, n, SSM_STATE)

    lr, li, bbr, bbi = pl.pallas_call(
        _ssm_param_kernel,
        out_shape=(mat, mat, cube, cube),
        name="ssm_params",
    )(log_dt.reshape(n, 1), a_re.reshape(n, SSM_STATE), a_im.reshape(n, SSM_STATE),
      channel_major(b_re), channel_major(b_im))
    shape3 = (DEPTH, SSM_GROUPS, SSM_STATE)
    return (lr.reshape(shape3), li.reshape(shape3),
            bbr.reshape((SSM_GROUP,) + shape3), bbi.reshape((SSM_GROUP,) + shape3))


N_SLAB = SSM_WIDTH // LANES
GROUPS_PER_SLAB = LANES // SSM_GROUP
STATES_PER_SLAB = GROUPS_PER_SLAB * SSM_STATE
N_STATES = SSM_GROUPS * SSM_STATE
SCAN_WIDTH = 1024


def _ssm_layout(lr, li, bbr, bbi, c_re, c_im):
    eye = jnp.eye(GROUPS_PER_SLAB, dtype=F32)

    def in_blocks(bb):
        t = bb.reshape(SSM_GROUP, DEPTH, N_SLAB, GROUPS_PER_SLAB, SSM_STATE).transpose(1, 2, 3, 0, 4)
        blk = eye[None, None, :, None, :, None] * t[:, :, :, :, None, :]
        return blk.reshape(DEPTH, N_SLAB, LANES, STATES_PER_SLAB)

    def out_blocks(cc):
        t = cc.reshape(DEPTH, N_SLAB, GROUPS_PER_SLAB, SSM_GROUP, SSM_STATE).transpose(0, 1, 2, 4, 3)
        blk = eye[None, None, :, None, :, None] * t[:, :, :, :, None, :]
        return blk.reshape(DEPTH, N_SLAB, STATES_PER_SLAB, LANES)

    bb = jnp.concatenate([in_blocks(bbr), in_blocks(bbi)], axis=-1).astype(BF16)
    lam = jnp.stack([lr.reshape(DEPTH, N_STATES), li.reshape(DEPTH, N_STATES)], axis=1)
    return bb, out_blocks(c_re).astype(BF16), out_blocks(c_im).astype(BF16), lam


def _ssm_kernel(u_ref, bb_ref, cre_ref, cim_ref, lam_ref, dsk_ref, wglu_ref, bglu_ref,
                o_ref, x_scr, h_scr, *stage):
    u_stage, o_stage = stage[:N_SLAB], stage[N_SLAB:]

    @pl.when(pl.program_id(0) == 0)
    def _():
        h_scr[...] = jnp.zeros_like(h_scr)

    for b in range(BATCH):
        u_b = u_ref[b].astype(F32)
        for s in range(N_SLAB):
            u_stage[s][pl.ds(b, SSM_CHUNK, stride=BATCH), :] = u_b[:, s * LANES:(s + 1) * LANES]
    u_tm = jnp.concatenate([u_stage[s][...] for s in range(N_SLAB)], axis=-1)
    u_tm_b = u_tm.astype(BF16)

    for s in range(N_SLAB):
        xs = jnp.dot(u_tm_b[:, s * LANES:(s + 1) * LANES], bb_ref[s], preferred_element_type=F32)
        x_scr[:, s * STATES_PER_SLAB:(s + 1) * STATES_PER_SLAB] = xs[:, :STATES_PER_SLAB]
        x_scr[:, N_STATES + s * STATES_PER_SLAB:N_STATES + (s + 1) * STATES_PER_SLAB] = xs[:, STATES_PER_SLAB:]

    width = SCAN_WIDTH
    for s in range(N_STATES // width):
        re_cols = pl.ds(s * width, width)
        im_cols = pl.ds(N_STATES + s * width, width)
        lr = jnp.broadcast_to(lam_ref[0:1, s * width:(s + 1) * width], (BATCH, width))
        li = jnp.broadcast_to(lam_ref[1:2, s * width:(s + 1) * width], (BATCH, width))

        def step(t, carry, re_cols=re_cols, im_cols=im_cols, lr=lr, li=li):
            hr, hi = carry
            r0 = pl.multiple_of(t * BATCH, BATCH)
            xr = x_scr[pl.ds(r0, BATCH), re_cols]
            xi = x_scr[pl.ds(r0, BATCH), im_cols]
            nr = lr * hr - li * hi + xr
            ni = lr * hi + li * hr + xi
            x_scr[pl.ds(r0, BATCH), re_cols] = nr
            x_scr[pl.ds(r0, BATCH), im_cols] = ni
            return nr, ni

        hr, hi = lax.fori_loop(0, SSM_CHUNK, step, (h_scr[:, re_cols], h_scr[:, im_cols]), unroll=4)
        h_scr[:, re_cols] = hr
        h_scr[:, im_cols] = hi

    ys = []
    for s in range(N_SLAB):
        h_re = x_scr[:, s * STATES_PER_SLAB:(s + 1) * STATES_PER_SLAB].astype(BF16)
        h_im = x_scr[:, N_STATES + s * STATES_PER_SLAB:N_STATES + (s + 1) * STATES_PER_SLAB].astype(BF16)
        ys.append(jnp.dot(h_re, cre_ref[s], preferred_element_type=F32)
                  - jnp.dot(h_im, cim_ref[s], preferred_element_type=F32))
    y = jnp.concatenate(ys, axis=-1) + dsk_ref[...] * u_tm
    g = jax.nn.gelu(y)
    gate = jnp.dot(g.astype(BF16), wglu_ref[...], preferred_element_type=F32) + bglu_ref[...]
    out_tm = g * _sigmoid(gate)
    for s in range(N_SLAB):
        o_stage[s][...] = out_tm[:, s * LANES:(s + 1) * LANES]
    for b in range(BATCH):
        rows_b = [o_stage[s][pl.ds(b, SSM_CHUNK, stride=BATCH), :] for s in range(N_SLAB)]
        o_ref[b] = jnp.concatenate(rows_b, axis=-1).astype(BF16)


def _ssm_mixer(l, proj3, bb, cre, cim, lam, d_skip3, w_glu, b_glu3):
    rows = BATCH * SSM_CHUNK
    return pl.pallas_call(
        _ssm_kernel,
        out_shape=jax.ShapeDtypeStruct((BATCH, SEQ, SSM_WIDTH), BF16),
        grid=(SEQ // SSM_CHUNK,),
        in_specs=[
            pl.BlockSpec((BATCH, SSM_CHUNK, SSM_WIDTH), lambda c: (0, c, OFF_U // SSM_WIDTH)),
            _resident((None, N_SLAB, LANES, 2 * STATES_PER_SLAB), lambda c: (l, 0, 0, 0)),
            _resident((None, N_SLAB, STATES_PER_SLAB, LANES), lambda c: (l, 0, 0, 0)),
            _resident((None, N_SLAB, STATES_PER_SLAB, LANES), lambda c: (l, 0, 0, 0)),
            _resident((None, 2, N_STATES), lambda c: (l, 0, 0)),
            _resident((None, 1, SSM_WIDTH), lambda c: (l, 0, 0)),
            _resident((None, SSM_WIDTH, SSM_WIDTH), lambda c: (l, 0, 0)),
            _resident((None, 1, SSM_WIDTH), lambda c: (l, 0, 0)),
        ],
        out_specs=pl.BlockSpec((BATCH, SSM_CHUNK, SSM_WIDTH), lambda c: (0, c, 0)),
        scratch_shapes=[pltpu.VMEM((rows, 2 * N_STATES), F32), pltpu.VMEM((BATCH, 2 * N_STATES), F32)]
        + [pltpu.VMEM((rows, LANES), F32)] * (2 * N_SLAB),
        compiler_params=_cparams(("arbitrary",)),
        name="ssm_mixer",
    )(proj3, bb, cre, cim, lam, d_skip3, w_glu, b_glu3)


ATTN_UNROLL_LATER = {1: 5, 4: 6}
ATTN_UNROLL_FIRST = 8


def _attn_kernel(slopes_ref, q0, k0, v0, q1, k1, v1, q2, k2, v2, o_ref,
                 qp, kt, va, vb, o0, l0, o1, l1, o2, l2):
    pair = pl.program_id(1)
    n_qblk = SEQ // QBLK
    lane = lax.broadcasted_iota(jnp.int32, (QBLK, LANES), 1)
    head0 = lane < HEAD_DIM
    qi = lax.broadcasted_iota(jnp.int32, (QBLK, 2 * QBLK), 0)
    kj = lax.broadcasted_iota(jnp.int32, (QBLK, 2 * QBLK), 1)
    dist = QBLK + qi - kj
    valid = (dist >= 0) & (dist <= QBLK)
    dist = dist.astype(F32)

    groups = ((q0, k0, v0, o0, l0), (q1, k1, v1, o1, l1), (q2, k2, v2, o2, l2))
    for g, (_, dil) in enumerate(DSWA_PATTERNS):
        q_ref, k_ref, v_ref, o_scr, l_scr = groups[g]
        n_blocks = SEQ // (dil * QBLK)

        def rows(start, dil=dil):
            if dil == 1:
                return pl.ds(pl.multiple_of(start, QBLK), QBLK)
            return pl.ds(start, QBLK, stride=dil)

        def natural_rows(idx, rows=rows, n_blocks=n_blocks, dil=dil):
            return rows(dil * QBLK * (idx % n_blocks) + idx // n_blocks)

        def reorder(idx, gather, carry=0):
            dst = pl.ds(pl.multiple_of(idx * QBLK, QBLK), QBLK)
            qp[dst, :] = (gather(q_ref) * (HEAD_DIM ** -0.5)).astype(BF16)
            kt[idx] = gather(k_ref).T.astype(BF16)
            v = gather(v_ref)
            va[dst, :] = jnp.where(head0, v, 1.0).astype(BF16)
            vb[dst, :] = jnp.where(head0, 1.0, v).astype(BF16)
            return carry

        if dil == 1:
            def reorder_natural(idx, carry, reorder=reorder):
                src = pl.ds(pl.multiple_of(idx * QBLK, QBLK), QBLK)
                return reorder(idx, lambda ref: ref[src, :].astype(F32), carry)

            lax.fori_loop(0, n_qblk, reorder_natural, 0, unroll=4)
        else:
            for idx in range(n_qblk):
                first_token = dil * QBLK * (idx % n_blocks) + idx // n_blocks

                def gather_strided(ref, first_token=first_token, dil=dil):
                    words = ref.bitcast(jnp.uint32)[pl.ds(first_token // 2, QBLK, stride=dil // 2), :]
                    bits = (words << 16) if first_token % 2 == 0 else (words & jnp.uint32(0xFFFF0000))
                    return pltpu.bitcast(bits, F32)

                reorder(idx, gather_strided)

        bias = []
        for hh in range(2):
            slope = slopes_ref[g * HEADS_PER_PATTERN + 2 * pair + hh] * float(dil)
            bias.append(jnp.where(valid, -slope * dist, NEG_INF))

        def block(idx, with_prev, o_scr=o_scr, l_scr=l_scr, bias=bias, natural_rows=natural_rows):
            cur = pl.ds(pl.multiple_of(idx * QBLK, QBLK), QBLK)
            q = qp[cur, :]
            if with_prev:
                keys_t = jnp.concatenate([kt[idx - 1], kt[idx]], axis=1)
                kv_rows = pl.ds(pl.multiple_of((idx - 1) * QBLK, QBLK), 2 * QBLK)
            else:
                keys_t = kt[idx]
                kv_rows = cur
            pvs, ms = [], []
            for hh, v_scr in ((0, va), (1, vb)):
                qh = jnp.where(head0 == (hh == 0), q, jnp.zeros_like(q))
                s = jnp.dot(qh, keys_t, preferred_element_type=F32)
                s = s + (bias[hh] if with_prev else bias[hh][:, QBLK:])
                m = jnp.max(s, axis=-1, keepdims=True)
                p = jnp.exp(s - m).astype(BF16)
                pvs.append(jnp.dot(p, v_scr[kv_rows, :], preferred_element_type=F32))
                ms.append(m)
            num = jnp.where(head0, pvs[0], pvs[1])
            den = pltpu.roll(jnp.where(head0, pvs[1], pvs[0]), HEAD_DIM, axis=1)
            dst = natural_rows(idx)
            o_scr[dst, :] = num / den
            l_scr[dst, :] = jnp.where(head0, ms[0], ms[1]) + jnp.log(den)

        def first_blocks(res, carry, block=block, n_blocks=n_blocks):
            block(res * n_blocks, False)
            return carry

        lax.fori_loop(0, dil, first_blocks, 0, unroll=min(dil, ATTN_UNROLL_FIRST))

        if n_blocks > 1:
            later = n_blocks - 1

            def later_blocks(k, carry, block=block, later=later, n_blocks=n_blocks):
                block((k // later) * n_blocks + k % later + 1, True)
                return carry

            lax.fori_loop(0, dil * later, later_blocks, 0, unroll=ATTN_UNROLL_LATER[dil])

    def merge(k, carry):
        r = pl.ds(pl.multiple_of(k * QBLK, QBLK), QBLK)
        la, lb, lc = l0[r, :], l1[r, :], l2[r, :]
        m = jnp.maximum(jnp.maximum(la, lb), lc)
        wa, wb, wc = jnp.exp(la - m), jnp.exp(lb - m), jnp.exp(lc - m)
        tot = wa + wb + wc
        o_ref[r, :] = ((wa * o0[r, :] + wb * o1[r, :] + wc * o2[r, :]) / tot).astype(BF16)
        return carry

    lax.fori_loop(0, n_qblk, merge, 0, unroll=2)


def _attention(proj, slopes):
    pairs = HEADS_PER_PATTERN // 2

    def col(base, g):
        return lambda b, p: (b, (base + g * HEADS_PER_PATTERN * HEAD_DIM) // LANES + p)

    in_specs = [pl.BlockSpec(memory_space=pltpu.SMEM)]
    for g in range(len(DSWA_PATTERNS)):
        for base in (OFF_Q, OFF_K, OFF_V):
            in_specs.append(pl.BlockSpec((SEQ, LANES), col(base, g)))
    scr = pltpu.VMEM((SEQ, LANES), F32)
    seq_b = pltpu.VMEM((SEQ, LANES), BF16)
    kt_b = pltpu.VMEM((SEQ // QBLK, LANES, QBLK), BF16)
    return pl.pallas_call(
        _attn_kernel,
        out_shape=jax.ShapeDtypeStruct((TOKENS, ATTN_WIDTH), BF16),
        grid=(BATCH, pairs),
        in_specs=in_specs,
        out_specs=pl.BlockSpec((SEQ, LANES), lambda b, p: (b, p)),
        scratch_shapes=[seq_b, kt_b, seq_b, seq_b] + [scr] * 6,
        compiler_params=_cparams(("arbitrary", "arbitrary")),
        name="dilated_attention",
    )(slopes, *([proj] * 9))


N_MERGE_CHUNKS = D_MODEL // TN_MERGE


def _merge_kernel(*refs):
    (s_ref, a_ref, cb_ref, cc_ref, ch_ref, cch_ref, chh_ref), refs = refs[:7], refs[7:]
    gate_refs, refs = refs[:N_BRANCH * N_MERGE_CHUNKS], refs[N_BRANCH * N_MERGE_CHUNKS:]
    (x_ref, gt_ref, sh2_ref, sc2_ref, gpost_ref, gffn_ref, cw_ref, wss_ref, wat_ref, wcv_ref, wo_ref,
     o_ref, h2_ref, z_scr) = refs[:-MERGE_SUBTILES]
    m_scrs = refs[-MERGE_SUBTILES:]
    i = pl.program_id(0)
    tiles_per_seq = SEQ // TM_MERGE
    b = i // tiles_per_seq
    seq_start = (i % tiles_per_seq) == 0

    halo = cch_ref[...].astype(F32) * chh_ref[...].astype(F32)
    z_scr[0:CONV_HALO, :] = jnp.where(seq_start, 0.0, halo)
    z_scr[CONV_HALO:CONV_HALO + TM_MERGE, :] = cc_ref[...].astype(F32) * ch_ref[...].astype(F32)

    sub = TM_MERGE // MERGE_SUBTILES
    for k in range(MERGE_SUBTILES):
        r = pl.ds(k * sub, sub)
        m_scr = m_scrs[k]
        lead = CONV_HALO + k * sub
        conv = (cw_ref[0:1, :] * z_scr[pl.ds(lead, sub), :]
                + cw_ref[1:2, :] * z_scr[pl.ds(lead - 1, sub), :]
                + cw_ref[2:3, :] * z_scr[pl.ds(lead - 2, sub), :])
        cv = (cb_ref[r, :].astype(F32) * conv).astype(BF16)
        s = s_ref[r, :]
        a = a_ref[r, :]
        for c in range(N_MERGE_CHUNKS):
            cols = slice(c * TN_MERGE, (c + 1) * TN_MERGE)
            y_ssm = jnp.dot(s, wss_ref[:, cols], preferred_element_type=F32)
            y_att = jnp.dot(a, wat_ref[:, cols], preferred_element_type=F32)
            y_cv = jnp.dot(cv, wcv_ref[:, cols], preferred_element_type=F32)
            merged = (gate_refs[c][r, :].astype(F32) * y_ssm
                      + gate_refs[N_MERGE_CHUNKS + c][r, :].astype(F32) * y_att
                      + gate_refs[2 * N_MERGE_CHUNKS + c][r, :].astype(F32) * y_cv)
            m_scr[:, cols] = merged.astype(BF16)

        y = jnp.dot(m_scr[...], wo_ref[...], preferred_element_type=F32)
        ms = jnp.mean(y * y, axis=-1, keepdims=True)
        yn = y * lax.rsqrt(ms + RMS_EPS) * gpost_ref[...]
        x_new = x_ref[r, :] + gt_ref[pl.ds(b, 1), :] * yn
        o_ref[r, :] = x_new
        h2_ref[r, :] = _norm_mod(x_new, gffn_ref[...], sc2_ref[pl.ds(b, 1), :], sh2_ref[pl.ds(b, 1), :])


def _merge(l, s, a, proj, x, mod, g_post3, g_ffn3, conv_w, w_ssm_out, w_attn_out, w_conv_out, w_o):
    tm = TM_MERGE
    d = D_MODEL
    cw = CONV_WIDTH
    cb_blk = OFF_CONV // cw
    halo_blocks = tm // CONV_HALO

    def halo_map(col_blk):
        return lambda i: (jnp.maximum(i * halo_blocks - 1, 0), col_blk)

    def gate_spec(branch, chunk):
        blk = (OFF_GATE + branch * d) // TN_MERGE + chunk
        return pl.BlockSpec((tm, TN_MERGE), lambda i: (i, blk))

    gate_specs = [gate_spec(br, c) for br in range(N_BRANCH) for c in range(N_MERGE_CHUNKS)]
    tile = pl.BlockSpec((tm, d), lambda i: (i, 0))
    return pl.pallas_call(
        _merge_kernel,
        out_shape=(jax.ShapeDtypeStruct((TOKENS, d), F32), jax.ShapeDtypeStruct((TOKENS, d), BF16)),
        grid=(TOKENS // tm,),
        in_specs=[
            pl.BlockSpec((tm, SSM_WIDTH), lambda i: (i, 0)),
            pl.BlockSpec((tm, ATTN_WIDTH), lambda i: (i, 0)),
            pl.BlockSpec((tm, cw), lambda i: (i, cb_blk)),
            pl.BlockSpec((tm, cw), lambda i: (i, cb_blk + 1)),
            pl.BlockSpec((tm, cw), lambda i: (i, cb_blk + 2)),
            pl.BlockSpec((CONV_HALO, cw), halo_map(cb_blk + 1)),
            pl.BlockSpec((CONV_HALO, cw), halo_map(cb_blk + 2)),
        ] + gate_specs + [
            tile,
            pl.BlockSpec((None, BATCH, d), lambda i: (l, 0, 2)),
            pl.BlockSpec((None, BATCH, d), lambda i: (l, 0, 3)),
            pl.BlockSpec((None, BATCH, d), lambda i: (l, 0, 4)),
            _resident((None, 1, d), lambda i: (l, 0, 0)),
            _resident((None, 1, d), lambda i: (l, 0, 0)),
            _resident((None, 3, cw), lambda i: (l, 0, 0)),
            _resident((None, SSM_WIDTH, d), lambda i: (l, 0, 0)),
            _resident((None, ATTN_WIDTH, d), lambda i: (l, 0, 0)),
            _resident((None, cw, d), lambda i: (l, 0, 0)),
            _resident((None, d, d), lambda i: (l, 0, 0)),
        ],
        out_specs=(tile, tile),
        scratch_shapes=[pltpu.VMEM((tm + CONV_HALO, cw), F32)]
        + [pltpu.VMEM((tm // MERGE_SUBTILES, d), BF16)] * MERGE_SUBTILES,
        compiler_params=_cparams(("arbitrary",)),
        name="branch_merge",
    )(s, a, *([proj] * (5 + len(gate_specs))), x, mod, mod, mod, g_post3, g_ffn3, conv_w,
      w_ssm_out, w_attn_out, w_conv_out, w_o)


def _up_kernel(h_ref, wa_ref, wb_ref, cwa_ref, cwb_ref, o_ref):
    half = SEQ // 2
    row = lax.broadcasted_iota(jnp.int32, (half, 1), 0)
    for first in (True, False):
        lead = 0 if first else UP_HALO
        h = h_ref[pl.ds(0 if first else half - UP_HALO, half + lead), :]
        convs = []
        for w_ref, cw_ref in ((wa_ref, cwa_ref), (wb_ref, cwb_ref)):
            z = jnp.dot(h, w_ref[...].astype(BF16), preferred_element_type=F32)
            z1 = pltpu.roll(z, 1, axis=0)
            z2 = pltpu.roll(z, 2, axis=0)
            if first:
                z1 = jnp.where(row >= 1, z1, 0.0)
                z2 = jnp.where(row >= 2, z2, 0.0)
            y = cw_ref[0:1, :] * z + cw_ref[1:2, :] * z1 + cw_ref[2:3, :] * z2
            convs.append(y[lead:, :])
        ca, cb = convs
        o_ref[pl.ds(0 if first else half, half), :] = (ca * _sigmoid(ca) * cb).astype(BF16)


def _mlp_up(l, h, w_up, conv_w):
    d = D_MODEL
    tf = TF_UP
    nf = D_FF // tf
    return pl.pallas_call(
        _up_kernel,
        out_shape=jax.ShapeDtypeStruct((TOKENS, D_FF), BF16),
        grid=(BATCH, nf),
        in_specs=[
            pl.BlockSpec((SEQ, d), lambda i, j: (i, 0)),
            pl.BlockSpec((None, d, tf), lambda i, j: (l, 0, j)),
            pl.BlockSpec((None, d, tf), lambda i, j: (l, 0, nf + j)),
            pl.BlockSpec((None, 3, tf), lambda i, j: (l, 0, j)),
            pl.BlockSpec((None, 3, tf), lambda i, j: (l, 0, nf + j)),
        ],
        out_specs=pl.BlockSpec((SEQ, tf), lambda i, j: (i, j)),
        compiler_params=_cparams(("arbitrary", "arbitrary")),
        name="mlp_up",
    )(h, w_up, w_up, conv_w, conv_w)


def _down_kernel(act_ref, w_ref, x_ref, gt_ref, gpost_ref, *rest):
    b = pl.program_id(0) // (SEQ // TM_DOWN)
    sub = TM_DOWN // DOWN_SUBTILES
    for k in range(DOWN_SUBTILES):
        r = pl.ds(k * sub, sub)
        y = jnp.dot(act_ref[r, :], w_ref[...], preferred_element_type=F32)
        ms = jnp.mean(y * y, axis=-1, keepdims=True)
        yn = y * lax.rsqrt(ms + RMS_EPS) * gpost_ref[...]
        x_new = x_ref[r, :] + gt_ref[pl.ds(b, 1), :] * yn
        if len(rest) == 1:
            (o_ref,) = rest
            o_ref[r, :] = x_new
        else:
            sh_ref, sc_ref, gnext_ref, o_ref, hn_ref = rest
            o_ref[r, :] = x_new
            hn_ref[r, :] = _norm_mod(x_new, gnext_ref[...], sc_ref[pl.ds(b, 1), :], sh_ref[pl.ds(b, 1), :])


def _mlp_down(l, act, w_down, x, mod, g_post3, g_mix3):
    d = D_MODEL
    tm = TM_DOWN
    last = l == DEPTH - 1
    tile = pl.BlockSpec((tm, d), lambda i: (i, 0))
    in_specs = [
        pl.BlockSpec((tm, D_FF), lambda i: (i, 0)),
        _resident((None, D_FF, d), lambda i: (l, 0, 0)),
        tile,
        pl.BlockSpec((None, BATCH, d), lambda i: (l, 0, 5)),
        _resident((None, 1, d), lambda i: (l, 0, 0)),
    ]
    args = [act, w_down, x, mod, g_post3]
    x_shape = jax.ShapeDtypeStruct((TOKENS, d), F32)
    if last:
        out_shape, out_specs = x_shape, tile
    else:
        in_specs += [
            pl.BlockSpec((None, BATCH, d), lambda i: (l + 1, 0, 0)),
            pl.BlockSpec((None, BATCH, d), lambda i: (l + 1, 0, 1)),
            _resident((None, 1, d), lambda i: (l + 1, 0, 0)),
        ]
        args += [mod, mod, g_mix3]
        out_shape, out_specs = (x_shape, jax.ShapeDtypeStruct((TOKENS, d), BF16)), (tile, tile)
    out = pl.pallas_call(
        _down_kernel,
        out_shape=out_shape,
        grid=(TOKENS // tm,),
        in_specs=in_specs,
        out_specs=out_specs,
        compiler_params=_cparams(("arbitrary",)),
        name="mlp_down",
    )(*args)
    return (out, None) if last else out


def _alibi_slopes():
    h = N_ATTN_HEADS
    return np.array([2.0 ** (-8.0 * (i + 1) / h) for i in range(h)], dtype=np.float32)


def kernel(x, c, w_mod, b_mod, g_pre_mix, g_post_mix, g_pre_ffn, g_post_ffn, w_in, ssm_log_dt, ssm_a_re,
           ssm_a_im, ssm_b_re, ssm_b_im, ssm_c_re, ssm_c_im, ssm_d, w_glu, b_glu, conv_mix_w, w_ssm_out,
           w_attn_out, w_conv_out, b_gate, w_o, w_up, ffn_conv_w, w_down):
    bias3 = _rows3(jnp.concatenate([jnp.zeros((DEPTH, OFF_GATE), F32), b_gate], axis=-1))
    w_glu_b = w_glu.astype(BF16)
    w_ssm_out_b = w_ssm_out.astype(BF16)
    w_attn_out_b = w_attn_out.astype(BF16)
    w_conv_out_b = w_conv_out.astype(BF16)
    w_o_b = w_o.astype(BF16)
    w_down_b = w_down.astype(BF16)
    g_pre_mix3, g_post_mix3 = _rows3(g_pre_mix), _rows3(g_post_mix)
    g_pre_ffn3, g_post_ffn3 = _rows3(g_pre_ffn), _rows3(g_post_ffn)
    d_skip3, b_glu3 = _rows3(ssm_d), _rows3(b_glu)

    mod = _modulation(c, w_mod, b_mod)
    lr, li, bbr, bbi = _ssm_params(ssm_log_dt, ssm_a_re, ssm_a_im, ssm_b_re, ssm_b_im)
    bb, cre, cim, lam = _ssm_layout(lr, li, bbr, bbi, ssm_c_re, ssm_c_im)
    slopes = jnp.asarray(_alibi_slopes())

    xf = x.reshape(TOKENS, D_MODEL)
    h = _first_norm(xf, g_pre_mix3, mod)
    for l in range(DEPTH):
        proj = _in_projection(l, h, w_in, bias3)
        s = _ssm_mixer(l, proj.reshape(BATCH, SEQ, N_IN), bb, cre, cim, lam,
                       d_skip3, w_glu_b, b_glu3)
        a = _attention(proj, slopes)
        xf, h = _merge(l, s.reshape(TOKENS, SSM_WIDTH), a, proj, xf, mod, g_post_mix3, g_pre_ffn3, conv_mix_w,
                       w_ssm_out_b, w_attn_out_b, w_conv_out_b, w_o_b)
        act = _mlp_up(l, h, w_up, ffn_conv_w)
        xf, h = _mlp_down(l, act, w_down_b, xf, mod, g_post_ffn3, g_pre_mix3)
    return xf.reshape(BATCH, SEQ, D_MODEL)
```

```python
import numpy as np
import jax
import jax.numpy as jnp
from jax import lax
from jax.experimental import pallas as pl
from jax.experimental.pallas import tpu as pltpu

F32 = jnp.float32
BF16 = jnp.bfloat16

D_MODEL = 2048
BATCH = 8
SEQ = 2048
DEPTH = 4
TOKENS = BATCH * SEQ
RMS_EPS = 1e-6
NEG_INF = -1e30
N_BRANCH = 3
SSM_WIDTH = 512
SSM_GROUP = 16
SSM_GROUPS = 32
SSM_STATE = 64
HEAD_DIM = 64
DSWA_PATTERNS = ((128, 1), (512, 4), (2048, 16))
ATTN_WIDTH = 512
HEADS_PER_PATTERN = 8
N_ATTN_HEADS = 24
QKV_WIDTH = 1536
CONV_WIDTH = 512
D_FF = 5632
OFF_U = 0
OFF_Q = SSM_WIDTH
OFF_K = OFF_Q + QKV_WIDTH
OFF_V = OFF_K + QKV_WIDTH
OFF_CONV = OFF_V + QKV_WIDTH
OFF_GATE = OFF_CONV + 3 * CONV_WIDTH
N_IN = OFF_GATE + N_BRANCH * D_MODEL

LANES = 128
VMEM_LIMIT = 62 * 1024 * 1024

TM_NORM = 1024
TN_PROJ = 1280
TF_UP = 512
UP_HALO = 16
TM_MERGE = 512
MERGE_SUBTILES = 2
CONV_HALO = 16
TN_MERGE = 512
TM_DOWN = 512
DOWN_SUBTILES = 2
SSM_CHUNK = 128
QBLK = 128


def _cparams(sem):
    return pltpu.CompilerParams(dimension_semantics=sem, vmem_limit_bytes=VMEM_LIMIT)


def _resident(shape, index_map):
    return pl.BlockSpec(shape, index_map, pipeline_mode=pl.Buffered(1))


def _sigmoid(x):
    return 0.5 * jnp.tanh(0.5 * x) + 0.5


def _rows3(a):
    return a.reshape(DEPTH, 1, a.shape[-1])


def _norm_mod(x, g, scale_row, shift_row):
    ms = jnp.mean(x * x, axis=-1, keepdims=True)
    y = x * lax.rsqrt(ms + RMS_EPS) * g
    return (y * (1.0 + scale_row) + shift_row).astype(BF16)


def _mod_kernel(c_ref, w_ref, b_ref, o_ref):
    c = c_ref[...]
    cond = (c * jax.nn.sigmoid(c)).astype(BF16)
    o_ref[0] = jnp.dot(cond, w_ref[0].astype(BF16), preferred_element_type=F32) + b_ref[0]


def _modulation(c, w_mod, b_mod):
    tn = 1024
    n = 6 * D_MODEL
    return pl.pallas_call(
        _mod_kernel,
        out_shape=jax.ShapeDtypeStruct((DEPTH, BATCH, n), F32),
        grid=(DEPTH, n // tn),
        in_specs=[
            pl.BlockSpec((BATCH, D_MODEL), lambda l, j: (0, 0)),
            pl.BlockSpec((1, D_MODEL, tn), lambda l, j: (l, 0, j)),
            pl.BlockSpec((1, 1, tn), lambda l, j: (l, 0, j)),
        ],
        out_specs=pl.BlockSpec((1, BATCH, tn), lambda l, j: (l, 0, j)),
        compiler_params=_cparams(("arbitrary", "arbitrary")),
        name="modulation",
    )(c, w_mod, b_mod.reshape(DEPTH, 1, n))


def _norm_kernel(x_ref, g_ref, sh_ref, sc_ref, o_ref):
    b = pl.program_id(0) // (SEQ // TM_NORM)
    rows = 128

    def body(k, carry):
        r = pl.ds(pl.multiple_of(k * rows, rows), rows)
        o_ref[r, :] = _norm_mod(x_ref[r, :], g_ref[...], sc_ref[pl.ds(b, 1), :], sh_ref[pl.ds(b, 1), :])
        return carry

    lax.fori_loop(0, TM_NORM // rows, body, 0)


def _first_norm(x, g3, mod):
    d = D_MODEL
    return pl.pallas_call(
        _norm_kernel,
        out_shape=jax.ShapeDtypeStruct((TOKENS, d), BF16),
        grid=(TOKENS // TM_NORM,),
        in_specs=[
            pl.BlockSpec((TM_NORM, d), lambda i: (i, 0)),
            pl.BlockSpec((None, 1, d), lambda i: (0, 0, 0)),
            pl.BlockSpec((None, BATCH, d), lambda i: (0, 0, 0)),
            pl.BlockSpec((None, BATCH, d), lambda i: (0, 0, 1)),
        ],
        out_specs=pl.BlockSpec((TM_NORM, d), lambda i: (i, 0)),
        compiler_params=_cparams(("arbitrary",)),
        name="first_norm",
    )(x, g3, mod, mod)


def _inproj_kernel(h_ref, w_ref, bias_ref, o_ref):
    j = pl.program_id(1)
    raw_tiles = OFF_GATE // TN_PROJ
    raw_cols_in_mixed = OFF_GATE - raw_tiles * TN_PROJ

    @pl.when(j < raw_tiles)
    def _():
        o_ref[...] = jnp.dot(h_ref[...], w_ref[...].astype(BF16), preferred_element_type=F32).astype(BF16)

    @pl.when(j == raw_tiles)
    def _():
        acc = jnp.dot(h_ref[...], w_ref[...].astype(BF16), preferred_element_type=F32)
        col = lax.broadcasted_iota(jnp.int32, acc.shape, 1)
        o_ref[...] = jnp.where(col < raw_cols_in_mixed, acc, _sigmoid(acc + bias_ref[...])).astype(BF16)

    @pl.when(j > raw_tiles)
    def _():
        acc = jnp.dot(h_ref[...], w_ref[...].astype(BF16), preferred_element_type=F32)
        o_ref[...] = _sigmoid(acc + bias_ref[...]).astype(BF16)


def _in_projection(l, h, w_in, bias3):
    d = D_MODEL
    return pl.pallas_call(
        _inproj_kernel,
        out_shape=jax.ShapeDtypeStruct((TOKENS, N_IN), BF16),
        grid=(BATCH, N_IN // TN_PROJ),
        in_specs=[
            pl.BlockSpec((SEQ, d), lambda i, j: (i, 0)),
            pl.BlockSpec((None, d, TN_PROJ), lambda i, j: (l, 0, j)),
            pl.BlockSpec((None, 1, TN_PROJ), lambda i, j: (l, 0, j)),
        ],
        out_specs=pl.BlockSpec((SEQ, TN_PROJ), lambda i, j: (i, j)),
        compiler_params=_cparams(("arbitrary", "arbitrary")),
        name="in_projection",
    )(h, w_in, bias3)


def _ssm_param_kernel(ldt_ref, ar_ref, ai_ref, br_ref, bi_ref, lr_ref, li_ref, bbr_ref, bbi_ref):
    dt = jnp.exp(ldt_ref[...])
    ar = ar_ref[...]
    ai = ai_ref[...]
    mag = jnp.exp(ar * dt)
    lr = mag * jnp.cos(ai * dt)
    li = mag * jnp.sin(ai * dt)
    den = ar * ar + ai * ai
    fr = ((lr - 1.0) * ar + li * ai) / den
    fi = (li * ar - (lr - 1.0) * ai) / den
    lr_ref[...] = lr
    li_ref[...] = li
    for ch in range(SSM_GROUP):
        br = br_ref[ch]
        bi = bi_ref[ch]
        bbr_ref[ch] = fr * br - fi * bi
        bbi_ref[ch] = fr * bi + fi * br


def _ssm_params(log_dt, a_re, a_im, b_re, b_im):
    n = DEPTH * SSM_GROUPS
    mat = jax.ShapeDtypeStruct((n, SSM_STATE), F32)
    cube = jax.ShapeDtypeStruct((SSM_GROUP, n, SSM_STATE), F32)

    def channel_major(b):
        return b.transpose(3, 0, 1, 2).reshape(SSM_GROUP, n, SSM_STATE)

    lr, li, bbr, bbi = pl.pallas_call(
        _ssm_param_kernel,
        out_shape=(mat, mat, cube, cube),
        name="ssm_params",
    )(log_dt.reshape(n, 1), a_re.reshape(n, SSM_STATE), a_im.reshape(n, SSM_STATE),
      channel_major(b_re), channel_major(b_im))
    shape3 = (DEPTH, SSM_GROUPS, SSM_STATE)
    return (lr.reshape(shape3), li.reshape(shape3),
            bbr.reshape((SSM_GROUP,) + shape3), bbi.reshape((SSM_GROUP,) + shape3))


N_SLAB = SSM_WIDTH // LANES
GROUPS_PER_SLAB = LANES // SSM_GROUP
STATES_PER_SLAB = GROUPS_PER_SLAB * SSM_STATE
N_STATES = SSM_GROUPS * SSM_STATE
SCAN_WIDTH = 1024


def _ssm_layout(lr, li, bbr, bbi, c_re, c_im):
    eye = jnp.eye(GROUPS_PER_SLAB, dtype=F32)

    def in_blocks(bb):
        t = bb.reshape(SSM_GROUP, DEPTH, N_SLAB, GROUPS_PER_SLAB, SSM_STATE).transpose(1, 2, 3, 0, 4)
        blk = eye[None, None, :, None, :, None] * t[:, :, :, :, None, :]
        return blk.reshape(DEPTH, N_SLAB, LANES, STATES_PER_SLAB)

    def out_blocks(cc):
        t = cc.reshape(DEPTH, N_SLAB, GROUPS_PER_SLAB, SSM_GROUP, SSM_STATE).transpose(0, 1, 2, 4, 3)
        blk = eye[None, None, :, None, :, None] * t[:, :, :, :, None, :]
        return blk.reshape(DEPTH, N_SLAB, STATES_PER_SLAB, LANES)

    bb = jnp.concatenate([in_blocks(bbr), in_blocks(bbi)], axis=-1).astype(BF16)
    lam = jnp.stack([lr.reshape(DEPTH, N_STATES), li.reshape(DEPTH, N_STATES)], axis=1)
    return bb, out_blocks(c_re).astype(BF16), out_blocks(c_im).astype(BF16), lam


def _ssm_kernel(u_ref, bb_ref, cre_ref, cim_ref, lam_ref, dsk_ref, wglu_ref, bglu_ref,
                o_ref, x_scr, h_scr, *stage):
    u_stage, o_stage = stage[:N_SLAB], stage[N_SLAB:]

    @pl.when(pl.program_id(0) == 0)
    def _():
        h_scr[...] = jnp.zeros_like(h_scr)

    for b in range(BATCH):
        u_b = u_ref[b].astype(F32)
        for s in range(N_SLAB):
            u_stage[s][pl.ds(b, SSM_CHUNK, stride=BATCH), :] = u_b[:, s * LANES:(s + 1) * LANES]
    u_tm = jnp.concatenate([u_stage[s][...] for s in range(N_SLAB)], axis=-1)
    u_tm_b = u_tm.astype(BF16)

    for s in range(N_SLAB):
        xs = jnp.dot(u_tm_b[:, s * LANES:(s + 1) * LANES], bb_ref[s], preferred_element_type=F32)
        x_scr[:, s * STATES_PER_SLAB:(s + 1) * STATES_PER_SLAB] = xs[:, :STATES_PER_SLAB]
        x_scr[:, N_STATES + s * STATES_PER_SLAB:N_STATES + (s + 1) * STATES_PER_SLAB] = xs[:, STATES_PER_SLAB:]

    width = SCAN_WIDTH
    for s in range(N_STATES // width):
        re_cols = pl.ds(s * width, width)
        im_cols = pl.ds(N_STATES + s * width, width)
        lr = jnp.broadcast_to(lam_ref[0:1, s * width:(s + 1) * width], (BATCH, width))
        li = jnp.broadcast_to(lam_ref[1:2, s * width:(s + 1) * width], (BATCH, width))

        def step(t, carry, re_cols=re_cols, im_cols=im_cols, lr=lr, li=li):
            hr, hi = carry
            r0 = pl.multiple_of(t * BATCH, BATCH)
            xr = x_scr[pl.ds(r0, BATCH), re_cols]
            xi = x_scr[pl.ds(r0, BATCH), im_cols]
            nr = lr * hr - li * hi + xr
            ni = lr * hi + li * hr + xi
            x_scr[pl.ds(r0, BATCH), re_cols] = nr
            x_scr[pl.ds(r0, BATCH), im_cols] = ni
            return nr, ni

        hr, hi = lax.fori_loop(0, SSM_CHUNK, step, (h_scr[:, re_cols], h_scr[:, im_cols]), unroll=4)
        h_scr[:, re_cols] = hr
        h_scr[:, im_cols] = hi

    ys = []
    for s in range(N_SLAB):
        h_re = x_scr[:, s * STATES_PER_SLAB:(s + 1) * STATES_PER_SLAB].astype(BF16)
        h_im = x_scr[:, N_STATES + s * STATES_PER_SLAB:N_STATES + (s + 1) * STATES_PER_SLAB].astype(BF16)
        ys.append(jnp.dot(h_re, cre_ref[s], preferred_element_type=F32)
                  - jnp.dot(h_im, cim_ref[s], preferred_element_type=F32))
    y = jnp.concatenate(ys, axis=-1) + dsk_ref[...] * u_tm
    g = jax.nn.gelu(y)
    gate = jnp.dot(g.astype(BF16), wglu_ref[...], preferred_element_type=F32) + bglu_ref[...]
    out_tm = g * _sigmoid(gate)
    for s in range(N_SLAB):
        o_stage[s][...] = out_tm[:, s * LANES:(s + 1) * LANES]
    for b in range(BATCH):
        rows_b = [o_stage[s][pl.ds(b, SSM_CHUNK, stride=BATCH), :] for s in range(N_SLAB)]
        o_ref[b] = jnp.concatenate(rows_b, axis=-1).astype(BF16)


def _ssm_mixer(l, proj3, bb, cre, cim, lam, d_skip3, w_glu, b_glu3):
    rows = BATCH * SSM_CHUNK
    return pl.pallas_call(
        _ssm_kernel,
        out_shape=jax.ShapeDtypeStruct((BATCH, SEQ, SSM_WIDTH), BF16),
        grid=(SEQ // SSM_CHUNK,),
        in_specs=[
            pl.BlockSpec((BATCH, SSM_CHUNK, SSM_WIDTH), lambda c: (0, c, OFF_U // SSM_WIDTH)),
            _resident((None, N_SLAB, LANES, 2 * STATES_PER_SLAB), lambda c: (l, 0, 0, 0)),
            _resident((None, N_SLAB, STATES_PER_SLAB, LANES), lambda c: (l, 0, 0, 0)),
            _resident((None, N_SLAB, STATES_PER_SLAB, LANES), lambda c: (l, 0, 0, 0)),
            _resident((None, 2, N_STATES), lambda c: (l, 0, 0)),
            _resident((None, 1, SSM_WIDTH), lambda c: (l, 0, 0)),
            _resident((None, SSM_WIDTH, SSM_WIDTH), lambda c: (l, 0, 0)),
            _resident((None, 1, SSM_WIDTH), lambda c: (l, 0, 0)),
        ],
        out_specs=pl.BlockSpec((BATCH, SSM_CHUNK, SSM_WIDTH), lambda c: (0, c, 0)),
        scratch_shapes=[pltpu.VMEM((rows, 2 * N_STATES), F32), pltpu.VMEM((BATCH, 2 * N_STATES), F32)]
        + [pltpu.VMEM((rows, LANES), F32)] * (2 * N_SLAB),
        compiler_params=_cparams(("arbitrary",)),
        name="ssm_mixer",
    )(proj3, bb, cre, cim, lam, d_skip3, w_glu, b_glu3)


ATTN_UNROLL_LATER = {1: 5, 4: 6}
ATTN_UNROLL_FIRST = 8


def _attn_kernel(slopes_ref, q0, k0, v0, q1, k1, v1, q2, k2, v2, o_ref,
                 qp, kt, va, vb, o0, l0, o1, l1, o2, l2):
    pair = pl.program_id(1)
    n_qblk = SEQ // QBLK
    lane = lax.broadcasted_iota(jnp.int32, (QBLK, LANES), 1)
    head0 = lane < HEAD_DIM
    qi = lax.broadcasted_iota(jnp.int32, (QBLK, 2 * QBLK), 0)
    kj = lax.broadcasted_iota(jnp.int32, (QBLK, 2 * QBLK), 1)
    dist = QBLK + qi - kj
    valid = (dist >= 0) & (dist <= QBLK)
    dist = dist.astype(F32)

    groups = ((q0, k0, v0, o0, l0), (q1, k1, v1, o1, l1), (q2, k2, v2, o2, l2))
    for g, (_, dil) in enumerate(DSWA_PATTERNS):
        q_ref, k_ref, v_ref, o_scr, l_scr = groups[g]
        n_blocks = SEQ // (dil * QBLK)

        def rows(start, dil=dil):
            if dil == 1:
                return pl.ds(pl.multiple_of(start, QBLK), QBLK)
            return pl.ds(start, QBLK, stride=dil)

        def natural_rows(idx, rows=rows, n_blocks=n_blocks, dil=dil):
            return rows(dil * QBLK * (idx % n_blocks) + idx // n_blocks)

        def reorder(idx, gather, carry=0):
            dst = pl.ds(pl.multiple_of(idx * QBLK, QBLK), QBLK)
            qp[dst, :] = (gather(q_ref) * (HEAD_DIM ** -0.5)).astype(BF16)
            kt[idx] = gather(k_ref).T.astype(BF16)
            v = gather(v_ref)
            va[dst, :] = jnp.where(head0, v, 1.0).astype(BF16)
            vb[dst, :] = jnp.where(head0, 1.0, v).astype(BF16)
            return carry

        if dil == 1:
            def reorder_natural(idx, carry, reorder=reorder):
                src = pl.ds(pl.multiple_of(idx * QBLK, QBLK), QBLK)
                return reorder(idx, lambda ref: ref[src, :].astype(F32), carry)

            lax.fori_loop(0, n_qblk, reorder_natural, 0, unroll=4)
        else:
            for idx in range(n_qblk):
                first_token = dil * QBLK * (idx % n_blocks) + idx // n_blocks

                def gather_strided(ref, first_token=first_token, dil=dil):
                    words = ref.bitcast(jnp.uint32)[pl.ds(first_token // 2, QBLK, stride=dil // 2), :]
                    bits = (words << 16) if first_token % 2 == 0 else (words & jnp.uint32(0xFFFF0000))
                    return pltpu.bitcast(bits, F32)

                reorder(idx, gather_strided)

        bias = []
        for hh in range(2):
            slope = slopes_ref[g * HEADS_PER_PATTERN + 2 * pair + hh] * float(dil)
            bias.append(jnp.where(valid, -slope * dist, NEG_INF))

        def block(idx, with_prev, o_scr=o_scr, l_scr=l_scr, bias=bias, natural_rows=natural_rows):
            cur = pl.ds(pl.multiple_of(idx * QBLK, QBLK), QBLK)
            q = qp[cur, :]
            if with_prev:
                keys_t = jnp.concatenate([kt[idx - 1], kt[idx]], axis=1)
                kv_rows = pl.ds(pl.multiple_of((idx - 1) * QBLK, QBLK), 2 * QBLK)
            else:
                keys_t = kt[idx]
                kv_rows = cur
            pvs, ms = [], []
            for hh, v_scr in ((0, va), (1, vb)):
                qh = jnp.where(head0 == (hh == 0), q, jnp.zeros_like(q))
                s = jnp.dot(qh, keys_t, preferred_element_type=F32)
                s = s + (bias[hh] if with_prev else bias[hh][:, QBLK:])
                m = jnp.max(s, axis=-1, keepdims=True)
                p = jnp.exp(s - m).astype(BF16)
                pvs.append(jnp.dot(p, v_scr[kv_rows, :], preferred_element_type=F32))
                ms.append(m)
            num = jnp.where(head0, pvs[0], pvs[1])
            den = pltpu.roll(jnp.where(head0, pvs[1], pvs[0]), HEAD_DIM, axis=1)
            dst = natural_rows(idx)
            o_scr[dst, :] = num / den
            l_scr[dst, :] = jnp.where(head0, ms[0], ms[1]) + jnp.log(den)

        def first_blocks(res, carry, block=block, n_blocks=n_blocks):
            block(res * n_blocks, False)
            return carry

        lax.fori_loop(0, dil, first_blocks, 0, unroll=min(dil, ATTN_UNROLL_FIRST))

        if n_blocks > 1:
            later = n_blocks - 1

            def later_blocks(k, carry, block=block, later=later, n_blocks=n_blocks):
                block((k // later) * n_blocks + k % later + 1, True)
                return carry

            lax.fori_loop(0, dil * later, later_blocks, 0, unroll=ATTN_UNROLL_LATER[dil])

    def merge(k, carry):
        r = pl.ds(pl.multiple_of(k * QBLK, QBLK), QBLK)
        la, lb, lc = l0[r, :], l1[r, :], l2[r, :]
        m = jnp.maximum(jnp.maximum(la, lb), lc)
        wa, wb, wc = jnp.exp(la - m), jnp.exp(lb - m), jnp.exp(lc - m)
        tot = wa + wb + wc
        o_ref[r, :] = ((wa * o0[r, :] + wb * o1[r, :] + wc * o2[r, :]) / tot).astype(BF16)
        return carry

    lax.fori_loop(0, n_qblk, merge, 0, unroll=2)


def _attention(proj, slopes):
    pairs = HEADS_PER_PATTERN // 2

    def col(base, g):
        return lambda b, p: (b, (base + g * HEADS_PER_PATTERN * HEAD_DIM) // LANES + p)

    in_specs = [pl.BlockSpec(memory_space=pltpu.SMEM)]
    for g in range(len(DSWA_PATTERNS)):
        for base in (OFF_Q, OFF_K, OFF_V):
            in_specs.append(pl.BlockSpec((SEQ, LANES), col(base, g)))
    scr = pltpu.VMEM((SEQ, LANES), F32)
    seq_b = pltpu.VMEM((SEQ, LANES), BF16)
    kt_b = pltpu.VMEM((SEQ // QBLK, LANES, QBLK), BF16)
    return pl.pallas_call(
        _attn_kernel,
        out_shape=jax.ShapeDtypeStruct((TOKENS, ATTN_WIDTH), BF16),
        grid=(BATCH, pairs),
        in_specs=in_specs,
        out_specs=pl.BlockSpec((SEQ, LANES), lambda b, p: (b, p)),
        scratch_shapes=[seq_b, kt_b, seq_b, seq_b] + [scr] * 6,
        compiler_params=_cparams(("arbitrary", "arbitrary")),
        name="dilated_attention",
    )(slopes, *([proj] * 9))


N_MERGE_CHUNKS = D_MODEL // TN_MERGE


def _merge_kernel(*refs):
    (s_ref, a_ref, cb_ref, cc_ref, ch_ref, cch_ref, chh_ref), refs = refs[:7], refs[7:]
    gate_refs, refs = refs[:N_BRANCH * N_MERGE_CHUNKS], refs[N_BRANCH * N_MERGE_CHUNKS:]
    (x_ref, gt_ref, sh2_ref, sc2_ref, gpost_ref, gffn_ref, cw_ref, wss_ref, wat_ref, wcv_ref, wo_ref,
     o_ref, h2_ref, z_scr) = refs[:-MERGE_SUBTILES]
    m_scrs = refs[-MERGE_SUBTILES:]
    i = pl.program_id(0)
    tiles_per_seq = SEQ // TM_MERGE
    b = i // tiles_per_seq
    seq_start = (i % tiles_per_seq) == 0

    halo = cch_ref[...].astype(F32) * chh_ref[...].astype(F32)
    z_scr[0:CONV_HALO, :] = jnp.where(seq_start, 0.0, halo)
    z_scr[CONV_HALO:CONV_HALO + TM_MERGE, :] = cc_ref[...].astype(F32) * ch_ref[...].astype(F32)

    sub = TM_MERGE // MERGE_SUBTILES
    for k in range(MERGE_SUBTILES):
        r = pl.ds(k * sub, sub)
        m_scr = m_scrs[k]
        lead = CONV_HALO + k * sub
        conv = (cw_ref[0:1, :] * z_scr[pl.ds(lead, sub), :]
                + cw_ref[1:2, :] * z_scr[pl.ds(lead - 1, sub), :]
                + cw_ref[2:3, :] * z_scr[pl.ds(lead - 2, sub), :])
        cv = (cb_ref[r, :].astype(F32) * conv).astype(BF16)
        s = s_ref[r, :]
        a = a_ref[r, :]
        for c in range(N_MERGE_CHUNKS):
            cols = slice(c * TN_MERGE, (c + 1) * TN_MERGE)
            y_ssm = jnp.dot(s, wss_ref[:, cols], preferred_element_type=F32)
            y_att = jnp.dot(a, wat_ref[:, cols], preferred_element_type=F32)
            y_cv = jnp.dot(cv, wcv_ref[:, cols], preferred_element_type=F32)
            merged = (gate_refs[c][r, :].astype(F32) * y_ssm
                      + gate_refs[N_MERGE_CHUNKS + c][r, :].astype(F32) * y_att
                      + gate_refs[2 * N_MERGE_CHUNKS + c][r, :].astype(F32) * y_cv)
            m_scr[:, cols] = merged.astype(BF16)

        y = jnp.dot(m_scr[...], wo_ref[...], preferred_element_type=F32)
        ms = jnp.mean(y * y, axis=-1, keepdims=True)
        yn = y * lax.rsqrt(ms + RMS_EPS) * gpost_ref[...]
        x_new = x_ref[r, :] + gt_ref[pl.ds(b, 1), :] * yn
        o_ref[r, :] = x_new
        h2_ref[r, :] = _norm_mod(x_new, gffn_ref[...], sc2_ref[pl.ds(b, 1), :], sh2_ref[pl.ds(b, 1), :])


def _merge(l, s, a, proj, x, mod, g_post3, g_ffn3, conv_w, w_ssm_out, w_attn_out, w_conv_out, w_o):
    tm = TM_MERGE
    d = D_MODEL
    cw = CONV_WIDTH
    cb_blk = OFF_CONV // cw
    halo_blocks = tm // CONV_HALO

    def halo_map(col_blk):
        return lambda i: (jnp.maximum(i * halo_blocks - 1, 0), col_blk)

    def gate_spec(branch, chunk):
        blk = (OFF_GATE + branch * d) // TN_MERGE + chunk
        return pl.BlockSpec((tm, TN_MERGE), lambda i: (i, blk))

    gate_specs = [gate_spec(br, c) for br in range(N_BRANCH) for c in range(N_MERGE_CHUNKS)]
    tile = pl.BlockSpec((tm, d), lambda i: (i, 0))
    return pl.pallas_call(
        _merge_kernel,
        out_shape=(jax.ShapeDtypeStruct((TOKENS, d), F32), jax.ShapeDtypeStruct((TOKENS, d), BF16)),
        grid=(TOKENS // tm,),
        in_specs=[
            pl.BlockSpec((tm, SSM_WIDTH), lambda i: (i, 0)),
            pl.BlockSpec((tm, ATTN_WIDTH), lambda i: (i, 0)),
            pl.BlockSpec((tm, cw), lambda i: (i, cb_blk)),
            pl.BlockSpec((tm, cw), lambda i: (i, cb_blk + 1)),
            pl.BlockSpec((tm, cw), lambda i: (i, cb_blk + 2)),
            pl.BlockSpec((CONV_HALO, cw), halo_map(cb_blk + 1)),
            pl.BlockSpec((CONV_HALO, cw), halo_map(cb_blk + 2)),
        ] + gate_specs + [
            tile,
            pl.BlockSpec((None, BATCH, d), lambda i: (l, 0, 2)),
            pl.BlockSpec((None, BATCH, d), lambda i: (l, 0, 3)),
            pl.BlockSpec((None, BATCH, d), lambda i: (l, 0, 4)),
            _resident((None, 1, d), lambda i: (l, 0, 0)),
            _resident((None, 1, d), lambda i: (l, 0, 0)),
            _resident((None, 3, cw), lambda i: (l, 0, 0)),
            _resident((None, SSM_WIDTH, d), lambda i: (l, 0, 0)),
            _resident((None, ATTN_WIDTH, d), lambda i: (l, 0, 0)),
            _resident((None, cw, d), lambda i: (l, 0, 0)),
            _resident((None, d, d), lambda i: (l, 0, 0)),
        ],
        out_specs=(tile, tile),
        scratch_shapes=[pltpu.VMEM((tm + CONV_HALO, cw), F32)]
        + [pltpu.VMEM((tm // MERGE_SUBTILES, d), BF16)] * MERGE_SUBTILES,
        compiler_params=_cparams(("arbitrary",)),
        name="branch_merge",
    )(s, a, *([proj] * (5 + len(gate_specs))), x, mod, mod, mod, g_post3, g_ffn3, conv_w,
      w_ssm_out, w_attn_out, w_conv_out, w_o)


def _up_kernel(h_ref, wa_ref, wb_ref, cwa_ref, cwb_ref, wd_ref, o_ref, wd_bf16_ref):
    wd_bf16_ref[...] = wd_ref[...].astype(BF16)

    half = SEQ // 2
    row = lax.broadcasted_iota(jnp.int32, (half, 1), 0)
    for first in (True, False):
        lead = 0 if first else UP_HALO
        h = h_ref[pl.ds(0 if first else half - UP_HALO, half + lead), :]
        convs = []
        for w_ref, cw_ref in ((wa_ref, cwa_ref), (wb_ref, cwb_ref)):
            z = jnp.dot(h, w_ref[...].astype(BF16), preferred_element_type=F32)
            z1 = pltpu.roll(z, 1, axis=0)
            z2 = pltpu.roll(z, 2, axis=0)
            if first:
                z1 = jnp.where(row >= 1, z1, 0.0)
                z2 = jnp.where(row >= 2, z2, 0.0)
            y = cw_ref[0:1, :] * z + cw_ref[1:2, :] * z1 + cw_ref[2:3, :] * z2
            convs.append(y[lead:, :])
        ca, cb = convs
        o_ref[pl.ds(0 if first else half, half), :] = (ca * _sigmoid(ca) * cb).astype(BF16)


def _mlp_up(l, h, w_up, conv_w, w_down):
    d = D_MODEL
    tf = TF_UP
    nf = D_FF // tf
    wd_rows = D_FF // (BATCH * nf)
    return pl.pallas_call(
        _up_kernel,
        out_shape=(jax.ShapeDtypeStruct((TOKENS, D_FF), BF16), jax.ShapeDtypeStruct((D_FF, d), BF16)),
        grid=(BATCH, nf),
        in_specs=[
            pl.BlockSpec((SEQ, d), lambda i, j: (i, 0)),
            pl.BlockSpec((None, d, tf), lambda i, j: (l, 0, j)),
            pl.BlockSpec((None, d, tf), lambda i, j: (l, 0, nf + j)),
            pl.BlockSpec((None, 3, tf), lambda i, j: (l, 0, j)),
            pl.BlockSpec((None, 3, tf), lambda i, j: (l, 0, nf + j)),
            pl.BlockSpec((None, wd_rows, d), lambda i, j: (l, i * nf + j, 0)),
        ],
        out_specs=(pl.BlockSpec((SEQ, tf), lambda i, j: (i, j)),
                   pl.BlockSpec((wd_rows, d), lambda i, j: (i * nf + j, 0))),
        compiler_params=_cparams(("arbitrary", "arbitrary")),
        name="mlp_up",
    )(h, w_up, w_up, conv_w, conv_w, w_down)


def _down_kernel(act_ref, w_ref, x_ref, gt_ref, gpost_ref, *rest):
    b = pl.program_id(0) // (SEQ // TM_DOWN)
    sub = TM_DOWN // DOWN_SUBTILES
    for k in range(DOWN_SUBTILES):
        r = pl.ds(k * sub, sub)
        y = jnp.dot(act_ref[r, :], w_ref[...], preferred_element_type=F32)
        ms = jnp.mean(y * y, axis=-1, keepdims=True)
        yn = y * lax.rsqrt(ms + RMS_EPS) * gpost_ref[...]
        x_new = x_ref[r, :] + gt_ref[pl.ds(b, 1), :] * yn
        if len(rest) == 1:
            (o_ref,) = rest
            o_ref[r, :] = x_new
        else:
            sh_ref, sc_ref, gnext_ref, o_ref, hn_ref = rest
            o_ref[r, :] = x_new
            hn_ref[r, :] = _norm_mod(x_new, gnext_ref[...], sc_ref[pl.ds(b, 1), :], sh_ref[pl.ds(b, 1), :])


def _mlp_down(l, act, w_down, x, mod, g_post3, g_mix3):
    d = D_MODEL
    tm = TM_DOWN
    last = l == DEPTH - 1
    tile = pl.BlockSpec((tm, d), lambda i: (i, 0))
    in_specs = [
        pl.BlockSpec((tm, D_FF), lambda i: (i, 0)),
        _resident((D_FF, d), lambda i: (0, 0)),
        tile,
        pl.BlockSpec((None, BATCH, d), lambda i: (l, 0, 5)),
        _resident((None, 1, d), lambda i: (l, 0, 0)),
    ]
    args = [act, w_down, x, mod, g_post3]
    x_shape = jax.ShapeDtypeStruct((TOKENS, d), F32)
    if last:
        out_shape, out_specs = x_shape, tile
    else:
        in_specs += [
            pl.BlockSpec((None, BATCH, d), lambda i: (l + 1, 0, 0)),
            pl.BlockSpec((None, BATCH, d), lambda i: (l + 1, 0, 1)),
            _resident((None, 1, d), lambda i: (l + 1, 0, 0)),
        ]
        args += [mod, mod, g_mix3]
        out_shape, out_specs = (x_shape, jax.ShapeDtypeStruct((TOKENS, d), BF16)), (tile, tile)
    out = pl.pallas_call(
        _down_kernel,
        out_shape=out_shape,
        grid=(TOKENS // tm,),
        in_specs=in_specs,
        out_specs=out_specs,
        compiler_params=_cparams(("arbitrary",)),
        name="mlp_down",
    )(*args)
    return (out, None) if last else out


def _alibi_slopes():
    h = N_ATTN_HEADS
    return np.array([2.0 ** (-8.0 * (i + 1) / h) for i in range(h)], dtype=np.float32)


def kernel(x, c, w_mod, b_mod, g_pre_mix, g_post_mix, g_pre_ffn, g_post_ffn, w_in, ssm_log_dt, ssm_a_re,
           ssm_a_im, ssm_b_re, ssm_b_im, ssm_c_re, ssm_c_im, ssm_d, w_glu, b_glu, conv_mix_w, w_ssm_out,
           w_attn_out, w_conv_out, b_gate, w_o, w_up, ffn_conv_w, w_down):
    bias3 = _rows3(jnp.concatenate([jnp.zeros((DEPTH, OFF_GATE), F32), b_gate], axis=-1))
    w_glu_b = w_glu.astype(BF16)
    w_ssm_out_b = w_ssm_out.astype(BF16)
    w_attn_out_b = w_attn_out.astype(BF16)
    w_conv_out_b = w_conv_out.astype(BF16)
    w_o_b = w_o.astype(BF16)
    g_pre_mix3, g_post_mix3 = _rows3(g_pre_mix), _rows3(g_post_mix)
    g_pre_ffn3, g_post_ffn3 = _rows3(g_pre_ffn), _rows3(g_post_ffn)
    d_skip3, b_glu3 = _rows3(ssm_d), _rows3(b_glu)

    mod = _modulation(c, w_mod, b_mod)
    lr, li, bbr, bbi = _ssm_params(ssm_log_dt, ssm_a_re, ssm_a_im, ssm_b_re, ssm_b_im)
    bb, cre, cim, lam = _ssm_layout(lr, li, bbr, bbi, ssm_c_re, ssm_c_im)
    slopes = jnp.asarray(_alibi_slopes())

    xf = x.reshape(TOKENS, D_MODEL)
    h = _first_norm(xf, g_pre_mix3, mod)
    for l in range(DEPTH):
        proj = _in_projection(l, h, w_in, bias3)
        s = _ssm_mixer(l, proj.reshape(BATCH, SEQ, N_IN), bb, cre, cim, lam,
                       d_skip3, w_glu_b, b_glu3)
        a = _attention(proj, slopes)
        xf, h = _merge(l, s.reshape(TOKENS, SSM_WIDTH), a, proj, xf, mod, g_post_mix3, g_pre_ffn3, conv_mix_w,
                       w_ssm_out_b, w_attn_out_b, w_conv_out_b, w_o_b)
        act, w_down_b = _mlp_up(l, h, w_up, ffn_conv_w, w_down)
        xf, h = _mlp_down(l, act, w_down_b, xf, mod, g_post_ffn3, g_pre_mix3)
    return xf.reshape(BATCH, SEQ, D_MODEL)
```
